```python
import math
import jax, jax.numpy as jnp
from jax import lax
import numpy as np

D_MODEL = 4096
BATCH = 1
SEQ = 8192
DEPTH = 1

CHUNK = 64
GMLP_BLOCK = 128
GMLP_GROUPS = 16
GMLP_WIDTH = D_MODEL // 2
GMLP_GROUP_DIM = GMLP_WIDTH // GMLP_GROUPS
SB_HEAD_DIM = 128
SB_HEADS = (D_MODEL // 2) // SB_HEAD_DIM
SB_WIDTH = SB_HEADS * SB_HEAD_DIM
Q_BLOCK = 128
D_FF = 4 * D_MODEL
N_MOD = 6
EPS = 1e-6
IN_COLS = 2 * GMLP_WIDTH + 3 * SB_WIDTH + 2 * D_MODEL

kernel_name = "hybrid_gmlp_stickbreaking_block"


def rms_norm(x, g):
    xf = x.astype(jnp.float32)
    inv = lax.rsqrt(jnp.mean(xf * xf, axis=-1, keepdims=True) + EPS)
    return (xf * inv * g.astype(jnp.float32)).astype(x.dtype)


def layer_norm(x, g):
    xf = x.astype(jnp.float32)
    mu = jnp.mean(xf, axis=-1, keepdims=True)
    xc = xf - mu
    inv = lax.rsqrt(jnp.mean(xc * xc, axis=-1, keepdims=True) + EPS)
    return (xc * inv * g.astype(jnp.float32)).astype(x.dtype)


def gmlp_branch(zuv, g_v, w_s, b_s):
    B, S, _ = zuv.shape
    zuv = jax.nn.gelu(zuv, approximate=False)
    u, v = jnp.split(zuv, 2, axis=-1)
    v = layer_norm(v, g_v)
    nblk = S // GMLP_BLOCK
    v = v.reshape(B, nblk, GMLP_BLOCK, GMLP_GROUPS, GMLP_GROUP_DIM)
    pos = jnp.arange(GMLP_BLOCK)
    mask = (pos[:, None] // CHUNK) >= (pos[None, :] // CHUNK)
    w = jnp.where(mask[None], w_s, jnp.zeros((), w_s.dtype))
    y = jnp.einsum('gts,bnsgc->bntgc', w, v) + b_s.T[None, None, :, :, None]
    y = y.reshape(B, S, GMLP_WIDTH)
    return u * y


def stick_breaking_branch(q, k, v):
    B, S, _ = q.shape
    to_heads = lambda t: t.reshape(B, S, SB_HEADS, SB_HEAD_DIM).transpose(0, 2, 1, 3)
    qh = to_heads(q).astype(jnp.float32)
    kh = to_heads(k).astype(jnp.float32)
    vh = to_heads(v).astype(jnp.float32)
    nb = S // Q_BLOCK
    qb = qh.reshape(B, SB_HEADS, nb, Q_BLOCK, SB_HEAD_DIM).transpose(2, 0, 1, 3, 4)
    key_pos = jnp.arange(S)
    scale = 1.0 / math.sqrt(SB_HEAD_DIM)

    def one_block(args):
        q_blk, blk = args
        z = jnp.einsum('bhqd,bhkd->bhqk', q_blk, kh) * scale
        qpos = blk * Q_BLOCK + jnp.arange(Q_BLOCK)
        mask = key_pos[None, :] < qpos[:, None]
        log_beta = jax.nn.log_sigmoid(z)
        log_one_minus = jnp.where(mask, jax.nn.log_sigmoid(-z), 0.0)
        between = lax.cumsum(log_one_minus, axis=3, reverse=True) - log_one_minus
        a = jnp.where(mask, jnp.exp(log_beta + between), 0.0)
        return jnp.einsum('bhqk,bhkd->bhqd', a, vh)

    o = lax.map(one_block, (qb, jnp.arange(nb)))
    o = o.transpose(1, 0, 3, 2, 4).reshape(B, S, SB_WIDTH)
    return o.astype(q.dtype)


def setup_inputs(seed: int = 0) -> dict:
    key = jax.random.key(seed)
    ks = jax.random.split(key, 20)
    f32 = jnp.float32
    nrm = lambda k, shape, s: jax.random.normal(k, shape, f32) * s
    L = DEPTH
    return {
        "x": nrm(ks[0], (BATCH, SEQ, D_MODEL), 1.0),
        "c": nrm(ks[1], (BATCH, D_MODEL), 1.0),
        "w_ada": nrm(ks[2], (L, D_MODEL, N_MOD * D_MODEL), 0.2 * D_MODEL ** -0.5),
        "b_ada": nrm(ks[3], (L, N_MOD * D_MODEL), 0.05),
        "g_pre_mix": 1.0 + nrm(ks[4], (L, D_MODEL), 0.05),
        "w_in": nrm(ks[5], (L, D_MODEL, IN_COLS), D_MODEL ** -0.5),
        "g_v": 1.0 + nrm(ks[6], (L, GMLP_WIDTH), 0.05),
        "w_s": nrm(ks[7], (L, GMLP_GROUPS, GMLP_BLOCK, GMLP_BLOCK), GMLP_BLOCK ** -0.5),
        "b_s": 1.0 + nrm(ks[8], (L, GMLP_GROUPS, GMLP_BLOCK), 0.1),
        "w_proj_a": nrm(ks[9], (L, GMLP_WIDTH, D_MODEL), GMLP_WIDTH ** -0.5),
        "w_proj_b": nrm(ks[10], (L, SB_WIDTH, D_MODEL), SB_WIDTH ** -0.5),
        "w_o": nrm(ks[11], (L, D_MODEL, D_MODEL), D_MODEL ** -0.5),
        "g_post_mix": 1.0 + nrm(ks[12], (L, D_MODEL), 0.05),
        "g_pre_mlp": 1.0 + nrm(ks[13], (L, D_MODEL), 0.05),
        "w_ff1": nrm(ks[14], (L, D_MODEL, D_FF), D_MODEL ** -0.5),
        "w_ff2": nrm(ks[15], (L, D_FF, D_MODEL), D_FF ** -0.5),
        "g_post_mlp": 1.0 + nrm(ks[16], (L, D_MODEL), 0.05),
    }


def reference(x, c, w_ada, b_ada, g_pre_mix, w_in, g_v, w_s, b_s, w_proj_a,
              w_proj_b, w_o, g_post_mix, g_pre_mlp, w_ff1, w_ff2, g_post_mlp):
    cut = [int(i) for i in np.cumsum([2 * GMLP_WIDTH, SB_WIDTH, SB_WIDTH, SB_WIDTH, D_MODEL])]
    for l in range(DEPTH):
        mod = (jax.nn.silu(c) @ w_ada[l] + b_ada[l])[:, None, :]
        sh1, sc1, gt1, sh2, sc2, gt2 = jnp.split(mod, N_MOD, axis=-1)

        h = rms_norm(x, g_pre_mix[l]) * (1.0 + sc1) + sh1
        proj = h @ w_in[l]
        zuv, q, k, v, gate_a, gate_b = jnp.split(proj, cut, axis=-1)
        y_a = gmlp_branch(zuv, g_v[l], w_s[l], b_s[l]) @ w_proj_a[l]
        y_b = stick_breaking_branch(q, k, v) @ w_proj_b[l]
        mix = (jax.nn.sigmoid(gate_a) * y_a + jax.nn.sigmoid(gate_b) * y_b) @ w_o[l]
        x = x + gt1 * rms_norm(mix, g_post_mix[l])

        h2 = rms_norm(x, g_pre_mlp[l]) * (1.0 + sc2) + sh2
        ff = jnp.square(jax.nn.relu(h2 @ w_ff1[l])) @ w_ff2[l]
        x = x + gt2 * rms_norm(ff, g_post_mlp[l])
    return x
```

```python
import functools
import math

import jax
import jax.numpy as jnp
from jax import lax
from jax.experimental import pallas as pl
from jax.experimental.pallas import tpu as pltpu

F32 = jnp.float32
BF16 = jnp.bfloat16

D_MODEL = 4096
CHUNK = 64
GMLP_BLOCK = 128
GMLP_GROUPS = 16
GMLP_WIDTH = D_MODEL // 2
SB_HEAD_DIM = 128
SB_HEADS = (D_MODEL // 2) // SB_HEAD_DIM
SB_WIDTH = SB_HEADS * SB_HEAD_DIM
D_FF = 4 * D_MODEL
N_MOD = 6
EPS = 1e-6
LANES = 128
MIB = 1024 * 1024
VMEM_LIMIT = 56 * MIB

ATT_TQ = 256
ATT_TK = 256


def _params(semantics, vmem=VMEM_LIMIT):
    return pltpu.CompilerParams(dimension_semantics=semantics, vmem_limit_bytes=vmem)


def _ada_kernel(c_ref, w_ref, b_ref, o_ref):
    c = c_ref[...]
    s = c * jax.nn.sigmoid(c)
    s8 = jnp.broadcast_to(s, (8, s.shape[1])).astype(BF16)
    p = jnp.dot(s8, w_ref[...].astype(BF16), preferred_element_type=F32)
    o_ref[...] = p[0:1, :] + b_ref[...]


def _adaln(c, w, b, tn=1024):
    d, n = w.shape
    return pl.pallas_call(
        _ada_kernel,
        grid=(n // tn,),
        in_specs=[pl.BlockSpec((1, d), lambda j: (0, 0)),
                  pl.BlockSpec((d, tn), lambda j: (0, j)),
                  pl.BlockSpec((1, tn), lambda j: (0, j))],
        out_specs=pl.BlockSpec((1, tn), lambda j: (0, j)),
        out_shape=jax.ShapeDtypeStruct((1, n), F32),
        compiler_params=_params(("arbitrary",)),
        name="adaln",
    )(c, w, b)


def _rms(x, g):
    inv = lax.rsqrt(jnp.mean(x * x, axis=-1, keepdims=True) + EPS)
    return x * inv * g


def _prenorm_kernel(x_ref, g_ref, sc_ref, sh_ref, o_ref):
    h = _rms(x_ref[...], g_ref[...]) * (1.0 + sc_ref[...]) + sh_ref[...]
    o_ref[...] = h.astype(o_ref.dtype)


def _prenorm(x, g, sc, sh, tm=512):
    m, d = x.shape
    row = pl.BlockSpec((tm, d), lambda i: (i, 0))
    vec = pl.BlockSpec((1, d), lambda i: (0, 0))
    return pl.pallas_call(
        _prenorm_kernel,
        grid=(m // tm,),
        in_specs=[row, vec, vec, vec],
        out_specs=row,
        out_shape=jax.ShapeDtypeStruct((m, d), BF16),
        compiler_params=_params(("arbitrary",)),
        name="prenorm",
    )(x, g, sc, sh)


def _post_pre_kernel(x_ref, y_ref, gpost_ref, gate_ref, gpre_ref, sc_ref, sh_ref, x1_ref, h_ref):
    x1 = x_ref[...] + gate_ref[...] * _rms(y_ref[...], gpost_ref[...])
    x1_ref[...] = x1
    h = _rms(x1, gpre_ref[...]) * (1.0 + sc_ref[...]) + sh_ref[...]
    h_ref[...] = h.astype(h_ref.dtype)


def _post_pre(x, y, gpost, gate, gpre, sc, sh, tm=256):
    m, d = x.shape
    row = pl.BlockSpec((tm, d), lambda i: (i, 0))
    vec = pl.BlockSpec((1, d), lambda i: (0, 0))
    return pl.pallas_call(
        _post_pre_kernel,
        grid=(m // tm,),
        in_specs=[row, row, vec, vec, vec, vec, vec],
        out_specs=[row, row],
        out_shape=[jax.ShapeDtypeStruct((m, d), F32), jax.ShapeDtypeStruct((m, d), BF16)],
        compiler_params=_params(("arbitrary",)),
        name="post_pre",
    )(x, y, gpost, gate, gpre, sc, sh)


def _post_kernel(x_ref, y_ref, gpost_ref, gate_ref, o_ref):
    o_ref[...] = x_ref[...] + gate_ref[...] * _rms(y_ref[...], gpost_ref[...])


def _post(x, y, gpost, gate, tm=256):
    m, d = x.shape
    row = pl.BlockSpec((tm, d), lambda i: (i, 0))
    vec = pl.BlockSpec((1, d), lambda i: (0, 0))
    return pl.pallas_call(
        _post_kernel,
        grid=(m // tm,),
        in_specs=[row, row, vec, vec],
        out_specs=row,
        out_shape=jax.ShapeDtypeStruct((m, d), F32),
        compiler_params=_params(("arbitrary",)),
        name="post",
    )(x, y, gpost, gate)


def _gelu(p):
    return 0.5 * p * (1.0 + lax.erf(p * (1.0 / math.sqrt(2.0))))


def _mm_kernel(a_ref, w_ref, o_ref, *, epilogue):
    p = jnp.dot(a_ref[...], w_ref[...].astype(BF16), preferred_element_type=F32)
    o_ref[...] = epilogue(p, pl.program_id(1)).astype(o_ref.dtype)


def _matmul(a, w, *, n_out, col_off, epilogue, out_dtype, tm, tn, name):
    m, k = a.shape
    joff = col_off // tn
    return pl.pallas_call(
        functools.partial(_mm_kernel, epilogue=epilogue),
        grid=(m // tm, n_out // tn),
        in_specs=[pl.BlockSpec((tm, k), lambda i, j: (i, 0)),
                  pl.BlockSpec((k, tn), lambda i, j: (0, j + joff))],
        out_specs=pl.BlockSpec((tm, tn), lambda i, j: (i, j)),
        out_shape=jax.ShapeDtypeStruct((m, n_out), out_dtype),
        compiler_params=_params(("arbitrary", "arbitrary")),
        name=name,
    )(a, w)


def _mmk_kernel(a_ref, w_ref, o_ref):
    p = jnp.dot(a_ref[...], w_ref[...].astype(BF16), preferred_element_type=F32)
    k = pl.program_id(2)

    @pl.when(k == 0)
    def _():
        o_ref[...] = p

    @pl.when(k > 0)
    def _():
        o_ref[...] += p


def _matmul_ktiled(a, w, *, tm, tn, tk, name):
    m, k = a.shape
    n = w.shape[1]
    return pl.pallas_call(
        _mmk_kernel,
        grid=(m // tm, n // tn, k // tk),
        in_specs=[pl.BlockSpec((tm, tk), lambda i, j, kk: (i, kk)),
                  pl.BlockSpec((tk, tn), lambda i, j, kk: (kk, j))],
        out_specs=pl.BlockSpec((tm, tn), lambda i, j, kk: (i, j)),
        out_shape=jax.ShapeDtypeStruct((m, n), F32),
        compiler_params=_params(("arbitrary", "arbitrary", "arbitrary")),
        name=name,
    )(a, w)


def _merge_kernel(ga_ref, ob_ref, wa_ref, wb_ref, sa_ref, sb_ref, o_ref):
    ya = jnp.dot(ga_ref[...], wa_ref[...].astype(BF16), preferred_element_type=F32)
    yb = jnp.dot(ob_ref[...], wb_ref[...].astype(BF16), preferred_element_type=F32)
    mix = sa_ref[...].astype(F32) * ya + sb_ref[...].astype(F32) * yb
    o_ref[...] = mix.astype(o_ref.dtype)


def _merge(ga, ob, wa, wb, gates, tm=1024, tn=512):
    m, k = ga.shape
    n = wa.shape[1]
    nb = n // tn
    act = pl.BlockSpec((tm, k), lambda i, j: (i, 0))
    wsp = pl.BlockSpec((k, tn), lambda i, j: (0, j))
    return pl.pallas_call(
        _merge_kernel,
        grid=(m // tm, nb),
        in_specs=[act, act, wsp, wsp,
                  pl.BlockSpec((tm, tn), lambda i, j: (i, j)),
                  pl.BlockSpec((tm, tn), lambda i, j: (i, j + nb))],
        out_specs=pl.BlockSpec((tm, tn), lambda i, j: (i, j)),
        out_shape=jax.ShapeDtypeStruct((m, n), BF16),
        compiler_params=_params(("arbitrary", "arbitrary")),
        name="merge",
    )(ga, ob, wa, wb, gates, gates)


def _gmlp_kernel(u_ref, v_ref, gv_ref, ws_ref, bs_ref, o_ref):
    rows = u_ref.shape[0]
    v = v_ref[...].astype(F32)
    mu = jnp.mean(v, axis=-1, keepdims=True)
    vc = v - mu
    inv = lax.rsqrt(jnp.mean(vc * vc, axis=-1, keepdims=True) + EPS)
    vn = (vc * inv * gv_ref[...]).astype(BF16)
    t = lax.broadcasted_iota(jnp.int32, (GMLP_BLOCK, GMLP_BLOCK), 0)
    s = lax.broadcasted_iota(jnp.int32, (GMLP_BLOCK, GMLP_BLOCK), 1)
    chunk_causal = (t // CHUNK) >= (s // CHUNK)
    gd = GMLP_WIDTH // GMLP_GROUPS
    for g in range(GMLP_GROUPS):
        w = jnp.where(chunk_causal, ws_ref[g], 0.0).astype(BF16)
        bias = bs_ref[g]
        for b in range(rows // GMLP_BLOCK):
            r = slice(b * GMLP_BLOCK, (b + 1) * GMLP_BLOCK)
            c = slice(g * gd, (g + 1) * gd)
            y = jnp.dot(w, vn[r, c], preferred_element_type=F32) + bias
            o_ref[r, c] = (u_ref[r, c].astype(F32) * y).astype(o_ref.dtype)


def _gmlp(zuv, g_v, w_s, b_s, rows=256):
    m = zuv.shape[0]
    half = pl.BlockSpec((rows, GMLP_WIDTH), lambda i: (i, 0))
    return pl.pallas_call(
        _gmlp_kernel,
        grid=(m // rows,),
        in_specs=[half,
                  pl.BlockSpec((rows, GMLP_WIDTH), lambda i: (i, 1)),
                  pl.BlockSpec((1, GMLP_WIDTH), lambda i: (0, 0)),
                  pl.BlockSpec((GMLP_GROUPS, GMLP_BLOCK, GMLP_BLOCK), lambda i: (0, 0, 0)),
                  pl.BlockSpec((GMLP_GROUPS, GMLP_BLOCK, 1), lambda i: (0, 0, 0))],
        out_specs=half,
        out_shape=jax.ShapeDtypeStruct((m, GMLP_WIDTH), BF16),
        compiler_params=_params(("arbitrary",)),
        name="gmlp",
    )(zuv, zuv, g_v, w_s, b_s[:, :, None])


def _sb_block(q, kj, vj, carry, upper, mask):
    z = lax.dot_general(q, kj, (((1,), (1,)), ((), ())), preferred_element_type=F32)
    e = jnp.exp(-jnp.abs(z))
    log_beta = jnp.minimum(z, 0.0) - jnp.log(1.0 + e)
    lom = log_beta - z
    if mask is not None:
        lom = jnp.where(mask, lom, 0.0)
    hi = lom.astype(BF16)
    lo = (lom - hi.astype(F32)).astype(BF16)
    between = (jnp.dot(hi, upper, preferred_element_type=F32)
               + jnp.dot(lo, upper, preferred_element_type=F32))
    a = jnp.exp(log_beta + between + carry)
    if mask is not None:
        a = jnp.where(mask, a, 0.0)
    pv = jnp.dot(a.astype(BF16), vj, preferred_element_type=F32)
    return pv, jnp.sum(lom, axis=1, keepdims=True)


def _attn_kernel(q_ref, k_ref, v_ref, o_ref, acc_ref, carry_ref):
    i = pl.program_id(1)
    q = q_ref[...]
    row = lax.broadcasted_iota(jnp.int32, (ATT_TQ, ATT_TK), 0)
    col = lax.broadcasted_iota(jnp.int32, (ATT_TQ, ATT_TK), 1)
    upper = (row > col).astype(BF16)
    causal = col < row

    def kv(j):
        off = pl.multiple_of(j * ATT_TK, ATT_TK)
        return k_ref[pl.ds(off, ATT_TK), :], v_ref[pl.ds(off, ATT_TK), :]

    kd, vd = kv(i)
    pv, tot = _sb_block(q, kd, vd, jnp.zeros((ATT_TQ, 1), F32), upper, causal)
    acc_ref[...] = pv
    carry_ref[...] = tot

    def body(t, c):
        kj, vj = kv(i - 1 - t)
        pv, tot = _sb_block(q, kj, vj, carry_ref[...], upper, None)
        acc_ref[...] += pv
        carry_ref[...] += tot
        return c

    lax.fori_loop(0, i, body, 0)
    o_ref[...] = acc_ref[...].astype(o_ref.dtype)


def _attention(qkv):
    s = qkv.shape[0]
    assert ATT_TQ == ATT_TK and s % ATT_TQ == 0
    return pl.pallas_call(
        _attn_kernel,
        grid=(SB_HEADS, s // ATT_TQ),
        in_specs=[pl.BlockSpec((ATT_TQ, SB_HEAD_DIM), lambda h, i: (i, h)),
                  pl.BlockSpec((s, SB_HEAD_DIM), lambda h, i: (0, SB_HEADS + h)),
                  pl.BlockSpec((s, SB_HEAD_DIM), lambda h, i: (0, 2 * SB_HEADS + h))],
        out_specs=pl.BlockSpec((ATT_TQ, SB_HEAD_DIM), lambda h, i: (i, h)),
        out_shape=jax.ShapeDtypeStruct((s, SB_WIDTH), BF16),
        scratch_shapes=[pltpu.VMEM((ATT_TQ, SB_HEAD_DIM), F32),
                        pltpu.VMEM((ATT_TQ, 1), F32)],
        compiler_params=_params(("arbitrary", "arbitrary")),
        name="sb_attention",
    )(qkv, qkv, qkv)


def kernel(x, c, w_ada, b_ada, g_pre_mix, w_in, g_v, w_s, b_s, w_proj_a, w_proj_b, w_o,
           g_post_mix, g_pre_mlp, w_ff1, w_ff2, g_post_mlp):
    batch, seq, d = x.shape
    assert batch == 1 and d == D_MODEL
    depth = w_ada.shape[0]
    xs = x.reshape(seq, d)
    q_scale = 1.0 / math.sqrt(SB_HEAD_DIM)
    tn_in = 512
    q_tiles = SB_WIDTH // tn_in

    def qkv_epilogue(p, j):
        return p * jnp.where(j < q_tiles, q_scale, 1.0)

    for l in range(depth):
        mod = _adaln(c, w_ada[l], b_ada[l][None, :])
        sh1, sc1, gt1, sh2, sc2, gt2 = [mod[:, n * d:(n + 1) * d] for n in range(N_MOD)]
        row = lambda v: v[None, :]

        h = _prenorm(xs, row(g_pre_mix[l]), sc1, sh1)
        mm = functools.partial(_matmul, h, w_in[l], out_dtype=BF16, tm=1024, tn=tn_in)
        zuv = mm(n_out=2 * GMLP_WIDTH, col_off=0, epilogue=lambda p, j: _gelu(p), name="in_zuv")
        qkv = mm(n_out=3 * SB_WIDTH, col_off=2 * GMLP_WIDTH, epilogue=qkv_epilogue, name="in_qkv")
        gates = mm(n_out=2 * d, col_off=2 * GMLP_WIDTH + 3 * SB_WIDTH,
                   epilogue=lambda p, j: jax.nn.sigmoid(p), name="in_gates")

        ga = _gmlp(zuv, row(g_v[l]), w_s[l], b_s[l])
        ob = _attention(qkv)
        mixin = _merge(ga, ob, w_proj_a[l], w_proj_b[l], gates)
        mix = _matmul(mixin, w_o[l], n_out=d, col_off=0, epilogue=lambda p, j: p,
                      out_dtype=F32, tm=1024, tn=512, name="w_o")
        xs, h2 = _post_pre(xs, mix, row(g_post_mix[l]), gt1, row(g_pre_mlp[l]), sc2, sh2)

        hid = _matmul(h2, w_ff1[l], n_out=D_FF, col_off=0,
                      epilogue=lambda p, j: jnp.square(jnp.maximum(p, 0.0)),
                      out_dtype=BF16, tm=1024, tn=512, name="ff1")
        ff = _matmul_ktiled(hid, w_ff2[l], tm=2048, tn=1024, tk=1024, name="ff2")
        xs = _post(xs, ff, row(g_post_mlp[l]), gt2)
    return xs.reshape(batch, seq, d)
```

```python
import functools
import math

import jax
import jax.numpy as jnp
from jax import lax
from jax.experimental import pallas as pl
from jax.experimental.pallas import tpu as pltpu

F32 = jnp.float32
BF16 = jnp.bfloat16

D_MODEL = 4096
CHUNK = 64
GMLP_BLOCK = 128
GMLP_GROUPS = 16
GMLP_WIDTH = D_MODEL // 2
SB_HEAD_DIM = 128
SB_HEADS = (D_MODEL // 2) // SB_HEAD_DIM
SB_WIDTH = SB_HEADS * SB_HEAD_DIM
D_FF = 4 * D_MODEL
N_MOD = 6
EPS = 1e-6
LANES = 128
MIB = 1024 * 1024
VMEM_LIMIT = 56 * MIB

ATT_TQ = 256
ATT_TK = 256
ATT_HEADS_PER_STEP = 4
EXP_UNDERFLOW = 110.0


def _params(semantics, vmem=VMEM_LIMIT):
    return pltpu.CompilerParams(dimension_semantics=semantics, vmem_limit_bytes=vmem)


def _ada_kernel(c_ref, w_ref, b_ref, o_ref):
    c = c_ref[...]
    s = c * jax.nn.sigmoid(c)
    s8 = jnp.broadcast_to(s, (8, s.shape[1])).astype(BF16)
    p = jnp.dot(s8, w_ref[...].astype(BF16), preferred_element_type=F32)
    o_ref[...] = p[0:1, :] + b_ref[...]


def _adaln(c, w, b, tn=1024):
    d, n = w.shape
    return pl.pallas_call(
        _ada_kernel,
        grid=(n // tn,),
        in_specs=[pl.BlockSpec((1, d), lambda j: (0, 0)),
                  pl.BlockSpec((d, tn), lambda j: (0, j)),
                  pl.BlockSpec((1, tn), lambda j: (0, j))],
        out_specs=pl.BlockSpec((1, tn), lambda j: (0, j)),
        out_shape=jax.ShapeDtypeStruct((1, n), F32),
        compiler_params=_params(("arbitrary",)),
        name="adaln",
    )(c, w, b)


def _rms(x, g):
    inv = lax.rsqrt(jnp.mean(x * x, axis=-1, keepdims=True) + EPS)
    return x * inv * g


def _prenorm_kernel(x_ref, g_ref, sc_ref, sh_ref, o_ref):
    h = _rms(x_ref[...], g_ref[...]) * (1.0 + sc_ref[...]) + sh_ref[...]
    o_ref[...] = h.astype(o_ref.dtype)


def _prenorm(x, g, sc, sh, tm=512):
    m, d = x.shape
    row = pl.BlockSpec((tm, d), lambda i: (i, 0))
    vec = pl.BlockSpec((1, d), lambda i: (0, 0))
    return pl.pallas_call(
        _prenorm_kernel,
        grid=(m // tm,),
        in_specs=[row, vec, vec, vec],
        out_specs=row,
        out_shape=jax.ShapeDtypeStruct((m, d), BF16),
        compiler_params=_params(("arbitrary",)),
        name="prenorm",
    )(x, g, sc, sh)


def _post_pre_kernel(x_ref, y_ref, gpost_ref, gate_ref, gpre_ref, sc_ref, sh_ref, x1_ref, h_ref):
    x1 = x_ref[...] + gate_ref[...] * _rms(y_ref[...], gpost_ref[...])
    x1_ref[...] = x1
    h = _rms(x1, gpre_ref[...]) * (1.0 + sc_ref[...]) + sh_ref[...]
    h_ref[...] = h.astype(h_ref.dtype)


def _post_pre(x, y, gpost, gate, gpre, sc, sh, tm=256):
    m, d = x.shape
    row = pl.BlockSpec((tm, d), lambda i: (i, 0))
    vec = pl.BlockSpec((1, d), lambda i: (0, 0))
    return pl.pallas_call(
        _post_pre_kernel,
        grid=(m // tm,),
        in_specs=[row, row, vec, vec, vec, vec, vec],
        out_specs=[row, row],
        out_shape=[jax.ShapeDtypeStruct((m, d), F32), jax.ShapeDtypeStruct((m, d), BF16)],
        compiler_params=_params(("arbitrary",)),
        name="post_pre",
    )(x, y, gpost, gate, gpre, sc, sh)


def _post_kernel(x_ref, y_ref, gpost_ref, gate_ref, o_ref):
    o_ref[...] = x_ref[...] + gate_ref[...] * _rms(y_ref[...], gpost_ref[...])


def _post(x, y, gpost, gate, tm=256):
    m, d = x.shape
    row = pl.BlockSpec((tm, d), lambda i: (i, 0))
    vec = pl.BlockSpec((1, d), lambda i: (0, 0))
    return pl.pallas_call(
        _post_kernel,
        grid=(m // tm,),
        in_specs=[row, row, vec, vec],
        out_specs=row,
        out_shape=jax.ShapeDtypeStruct((m, d), F32),
        compiler_params=_params(("arbitrary",)),
        name="post",
    )(x, y, gpost, gate)


def _gelu(p):
    return 0.5 * p * (1.0 + lax.erf(p * (1.0 / math.sqrt(2.0))))


def _mm_kernel(a_ref, w_ref, o_ref, *, epilogue):
    p = jnp.dot(a_ref[...], w_ref[...].astype(BF16), preferred_element_type=F32)
    o_ref[...] = epilogue(p, pl.program_id(1)).astype(o_ref.dtype)


def _matmul(a, w, *, n_out, col_off, epilogue, out_dtype, tm, tn, name):
    m, k = a.shape
    joff = col_off // tn
    return pl.pallas_call(
        functools.partial(_mm_kernel, epilogue=epilogue),
        grid=(m // tm, n_out // tn),
        in_specs=[pl.BlockSpec((tm, k), lambda i, j: (i, 0)),
                  pl.BlockSpec((k, tn), lambda i, j: (0, j + joff))],
        out_specs=pl.BlockSpec((tm, tn), lambda i, j: (i, j)),
        out_shape=jax.ShapeDtypeStruct((m, n_out), out_dtype),
        compiler_params=_params(("arbitrary", "arbitrary")),
        name=name,
    )(a, w)


MMK_ROWS = 256


def _mmk_kernel(a_ref, w_ref, o_ref):
    @pl.when(pl.program_id(2) == 0)
    def _():
        o_ref[...] = jnp.zeros_like(o_ref)

    w = w_ref[...].astype(BF16)
    for r in range(a_ref.shape[0] // MMK_ROWS):
        rows = pl.ds(r * MMK_ROWS, MMK_ROWS)
        o_ref[rows, :] += jnp.dot(a_ref[rows, :], w, preferred_element_type=F32)


def _matmul_ktiled(a, w, *, tm, tn, tk, name):
    m, k = a.shape
    n = w.shape[1]
    return pl.pallas_call(
        _mmk_kernel,
        grid=(m // tm, n // tn, k // tk),
        in_specs=[pl.BlockSpec((tm, tk), lambda i, j, kk: (i, kk)),
                  pl.BlockSpec((tk, tn), lambda i, j, kk: (kk, j))],
        out_specs=pl.BlockSpec((tm, tn), lambda i, j, kk: (i, j)),
        out_shape=jax.ShapeDtypeStruct((m, n), F32),
        compiler_params=_params(("arbitrary", "arbitrary", "arbitrary")),
        name=name,
    )(a, w)


def _merge_kernel(ga_ref, ob_ref, wa_ref, wb_ref, sa_ref, sb_ref, o_ref):
    ya = jnp.dot(ga_ref[...], wa_ref[...].astype(BF16), preferred_element_type=F32)
    yb = jnp.dot(ob_ref[...], wb_ref[...].astype(BF16), preferred_element_type=F32)
    mix = sa_ref[...].astype(F32) * ya + sb_ref[...].astype(F32) * yb
    o_ref[...] = mix.astype(o_ref.dtype)


def _merge(ga, ob, wa, wb, gates, tm=1024, tn=512):
    m, k = ga.shape
    n = wa.shape[1]
    nb = n // tn
    act = pl.BlockSpec((tm, k), lambda i, j: (i, 0))
    wsp = pl.BlockSpec((k, tn), lambda i, j: (0, j))
    return pl.pallas_call(
        _merge_kernel,
        grid=(m // tm, nb),
        in_specs=[act, act, wsp, wsp,
                  pl.BlockSpec((tm, tn), lambda i, j: (i, j)),
                  pl.BlockSpec((tm, tn), lambda i, j: (i, j + nb))],
        out_specs=pl.BlockSpec((tm, tn), lambda i, j: (i, j)),
        out_shape=jax.ShapeDtypeStruct((m, n), BF16),
        compiler_params=_params(("arbitrary", "arbitrary")),
        name="merge",
    )(ga, ob, wa, wb, gates, gates)


def _gmlp_kernel(u_ref, v_ref, gv_ref, ws_ref, bs_ref, o_ref):
    rows = u_ref.shape[0]
    v = v_ref[...].astype(F32)
    mu = jnp.mean(v, axis=-1, keepdims=True)
    vc = v - mu
    inv = lax.rsqrt(jnp.mean(vc * vc, axis=-1, keepdims=True) + EPS)
    vn = (vc * inv * gv_ref[...]).astype(BF16)
    t = lax.broadcasted_iota(jnp.int32, (GMLP_BLOCK, GMLP_BLOCK), 0)
    s = lax.broadcasted_iota(jnp.int32, (GMLP_BLOCK, GMLP_BLOCK), 1)
    chunk_causal = (t // CHUNK) >= (s // CHUNK)
    gd = GMLP_WIDTH // GMLP_GROUPS
    for g in range(GMLP_GROUPS):
        w = jnp.where(chunk_causal, ws_ref[g], 0.0).astype(BF16)
        bias = bs_ref[g]
        for b in range(rows // GMLP_BLOCK):
            r = slice(b * GMLP_BLOCK, (b + 1) * GMLP_BLOCK)
            c = slice(g * gd, (g + 1) * gd)
            y = jnp.dot(w, vn[r, c], preferred_element_type=F32) + bias
            o_ref[r, c] = (u_ref[r, c].astype(F32) * y).astype(o_ref.dtype)


def _gmlp(zuv, g_v, w_s, b_s, rows=256):
    m = zuv.shape[0]
    half = pl.BlockSpec((rows, GMLP_WIDTH), lambda i: (i, 0))
    return pl.pallas_call(
        _gmlp_kernel,
        grid=(m // rows,),
        in_specs=[half,
                  pl.BlockSpec((rows, GMLP_WIDTH), lambda i: (i, 1)),
                  pl.BlockSpec((1, GMLP_WIDTH), lambda i: (0, 0)),
                  pl.BlockSpec((GMLP_GROUPS, GMLP_BLOCK, GMLP_BLOCK), lambda i: (0, 0, 0)),
                  pl.BlockSpec((GMLP_GROUPS, GMLP_BLOCK, 1), lambda i: (0, 0, 0))],
        out_specs=half,
        out_shape=jax.ShapeDtypeStruct((m, GMLP_WIDTH), BF16),
        compiler_params=_params(("arbitrary",)),
        name="gmlp",
    )(zuv, zuv, g_v, w_s, b_s[:, :, None])


def _sb_block(q, kj, vj, carry, upper, mask):
    z = lax.dot_general(q, kj, (((1,), (1,)), ((), ())), preferred_element_type=F32)
    e = jnp.exp(-jnp.abs(z))
    log_beta = jnp.minimum(z, 0.0) - jnp.log(1.0 + e)
    lom = log_beta - z
    if mask is not None:
        lom = jnp.where(mask, lom, 0.0)
    hi = lom.astype(BF16)
    lo = (lom - hi.astype(F32)).astype(BF16)
    between = (jnp.dot(hi, upper, preferred_element_type=F32)
               + jnp.dot(lo, upper, preferred_element_type=F32))
    a = jnp.exp(log_beta + between + carry)
    if mask is not None:
        a = jnp.where(mask, a, 0.0)
    pv = jnp.dot(a.astype(BF16), vj, preferred_element_type=F32)
    return pv, jnp.sum(lom, axis=1, keepdims=True)


def _attn_kernel(q_ref, k_ref, v_ref, o_ref, acc_ref, carry_ref):
    i = pl.program_id(1)
    row = lax.broadcasted_iota(jnp.int32, (ATT_TQ, ATT_TK), 0)
    col = lax.broadcasted_iota(jnp.int32, (ATT_TQ, ATT_TK), 1)
    upper = (row > col).astype(BF16)
    causal = col < row
    heads = [slice(a * SB_HEAD_DIM, (a + 1) * SB_HEAD_DIM) for a in range(ATT_HEADS_PER_STEP)]

    def tile(a, j, carry, mask):
        off = pl.multiple_of(j * ATT_TK, ATT_TK)
        return _sb_block(q_ref[:, heads[a]], k_ref[pl.ds(off, ATT_TK), heads[a]],
                         v_ref[pl.ds(off, ATT_TK), heads[a]], carry, upper, mask)

    no_carry = jnp.zeros((ATT_TQ, 1), F32)

    @pl.when(i == 0)
    def _():
        for a in range(ATT_HEADS_PER_STEP):
            pv, _ = tile(a, i, no_carry, causal)
            o_ref[:, heads[a]] = pv.astype(o_ref.dtype)

    @pl.when(i > 0)
    def _():
        for a in range(ATT_HEADS_PER_STEP):
            pv0, tot0 = tile(a, i, no_carry, causal)
            pv1, tot1 = tile(a, i - 1, tot0, None)
            acc_ref[a] = pv0 + pv1
            carry_ref[a] = tot0 + tot1

        def more(state):
            j, max_carry = state
            return jnp.logical_and(j >= 0, max_carry > -EXP_UNDERFLOW)

        def sweep(state):
            j, _ = state
            for a in range(ATT_HEADS_PER_STEP):
                pv, tot = tile(a, j, carry_ref[a], None)
                acc_ref[a] += pv
                carry_ref[a] += tot
            return j - 1, jnp.max(carry_ref[...])

        lax.while_loop(more, sweep, (i - 2, jnp.max(carry_ref[...])))
        for a in range(ATT_HEADS_PER_STEP):
            o_ref[:, heads[a]] = acc_ref[a].astype(o_ref.dtype)


def _attention(qkv):
    s = qkv.shape[0]
    assert ATT_TQ == ATT_TK and s % ATT_TQ == 0
    width = ATT_HEADS_PER_STEP * SB_HEAD_DIM
    groups = SB_HEADS // ATT_HEADS_PER_STEP
    return pl.pallas_call(
        _attn_kernel,
        grid=(groups, s // ATT_TQ),
        in_specs=[pl.BlockSpec((ATT_TQ, width), lambda h, i: (i, h)),
                  pl.BlockSpec((s, width), lambda h, i: (0, groups + h)),
                  pl.BlockSpec((s, width), lambda h, i: (0, 2 * groups + h))],
        out_specs=pl.BlockSpec((ATT_TQ, width), lambda h, i: (i, h)),
        out_shape=jax.ShapeDtypeStruct((s, SB_WIDTH), BF16),
        scratch_shapes=[pltpu.VMEM((ATT_HEADS_PER_STEP, ATT_TQ, SB_HEAD_DIM), F32),
                        pltpu.VMEM((ATT_HEADS_PER_STEP, ATT_TQ, 1), F32)],
        compiler_params=_params(("arbitrary", "arbitrary")),
        name="sb_attention",
    )(qkv, qkv, qkv)


def kernel(x, c, w_ada, b_ada, g_pre_mix, w_in, g_v, w_s, b_s, w_proj_a, w_proj_b, w_o,
           g_post_mix, g_pre_mlp, w_ff1, w_ff2, g_post_mlp):
    batch, seq, d = x.shape
    assert batch == 1 and d == D_MODEL
    depth = w_ada.shape[0]
    xs = x.reshape(seq, d)
    q_scale = 1.0 / math.sqrt(SB_HEAD_DIM)
    tn_in = 512
    q_tiles = SB_WIDTH // tn_in

    def qkv_epilogue(p, j):
        return p * jnp.where(j < q_tiles, q_scale, 1.0)

    for l in range(depth):
        mod = _adaln(c, w_ada[l], b_ada[l][None, :])
        sh1, sc1, gt1, sh2, sc2, gt2 = [mod[:, n * d:(n + 1) * d] for n in range(N_MOD)]
        row = lambda v: v[None, :]

        h = _prenorm(xs, row(g_pre_mix[l]), sc1, sh1)
        mm = functools.partial(_matmul, h, w_in[l], out_dtype=BF16, tm=1024, tn=tn_in)
        zuv = mm(n_out=2 * GMLP_WIDTH, col_off=0, epilogue=lambda p, j: _gelu(p), name="in_zuv")
        qkv = mm(n_out=3 * SB_WIDTH, col_off=2 * GMLP_WIDTH, epilogue=qkv_epilogue, name="in_qkv")
        gates = mm(n_out=2 * d, col_off=2 * GMLP_WIDTH + 3 * SB_WIDTH,
                   epilogue=lambda p, j: jax.nn.sigmoid(p), name="in_gates")

        ga = _gmlp(zuv, row(g_v[l]), w_s[l], b_s[l])
        ob = _attention(qkv)
        mixin = _merge(ga, ob, w_proj_a[l], w_proj_b[l], gates)
        mix = _matmul(mixin, w_o[l], n_out=d, col_off=0, epilogue=lambda p, j: p,
                      out_dtype=F32, tm=1024, tn=512, name="w_o")
        xs, h2 = _post_pre(xs, mix, row(g_post_mix[l]), gt1, row(g_pre_mlp[l]), sc2, sh2)

        hid = _matmul(h2, w_ff1[l], n_out=D_FF, col_off=0,
                      epilogue=lambda p, j: jnp.square(jnp.maximum(p, 0.0)),
                      out_dtype=BF16, tm=1024, tn=512, name="ff1")
        ff = _matmul_ktiled(hid, w_ff2[l], tm=2048, tn=1024, tk=1024, name="ff2")
        xs = _post(xs, ff, row(g_post_mlp[l]), gt2)
    return xs.reshape(batch, seq, d)
```

```python
import functools
import math

import jax
import jax.numpy as jnp
from jax import lax
from jax.experimental import pallas as pl
from jax.experimental.pallas import tpu as pltpu

F32 = jnp.float32
BF16 = jnp.bfloat16

D_MODEL = 4096
CHUNK = 64
GMLP_BLOCK = 128
GMLP_GROUPS = 16
GMLP_WIDTH = D_MODEL // 2
SB_HEAD_DIM = 128
SB_HEADS = (D_MODEL // 2) // SB_HEAD_DIM
SB_WIDTH = SB_HEADS * SB_HEAD_DIM
D_FF = 4 * D_MODEL
N_MOD = 6
EPS = 1e-6
LANES = 128
MIB = 1024 * 1024
VMEM_LIMIT = 56 * MIB

ATT_TQ = 256
ATT_TK = 256
ATT_HEADS_PER_STEP = 4
LOG2_E = 1.0 / math.log(2.0)
EXP2_UNDERFLOW = 160.0


def _params(semantics, vmem=VMEM_LIMIT):
    return pltpu.CompilerParams(dimension_semantics=semantics, vmem_limit_bytes=vmem)


def _ada_kernel(c_ref, w_ref, b_ref, o_ref):
    c = c_ref[...]
    s = c * jax.nn.sigmoid(c)
    s8 = jnp.broadcast_to(s, (8, s.shape[1])).astype(BF16)
    p = jnp.dot(s8, w_ref[...].astype(BF16), preferred_element_type=F32)
    o_ref[...] = p[0:1, :] + b_ref[...]


def _adaln(c, w, b, tn=1024):
    d, n = w.shape
    return pl.pallas_call(
        _ada_kernel,
        grid=(n // tn,),
        in_specs=[pl.BlockSpec((1, d), lambda j: (0, 0)),
                  pl.BlockSpec((d, tn), lambda j: (0, j)),
                  pl.BlockSpec((1, tn), lambda j: (0, j))],
        out_specs=pl.BlockSpec((1, tn), lambda j: (0, j)),
        out_shape=jax.ShapeDtypeStruct((1, n), F32),
        compiler_params=_params(("arbitrary",)),
        name="adaln",
    )(c, w, b)


def _rms(x, g):
    inv = lax.rsqrt(jnp.mean(x * x, axis=-1, keepdims=True) + EPS)
    return x * inv * g


def _prenorm_kernel(x_ref, g_ref, sc_ref, sh_ref, o_ref):
    h = _rms(x_ref[...], g_ref[...]) * (1.0 + sc_ref[...]) + sh_ref[...]
    o_ref[...] = h.astype(o_ref.dtype)


def _prenorm(x, g, sc, sh, tm=512):
    m, d = x.shape
    row = pl.BlockSpec((tm, d), lambda i: (i, 0))
    vec = pl.BlockSpec((1, d), lambda i: (0, 0))
    return pl.pallas_call(
        _prenorm_kernel,
        grid=(m // tm,),
        in_specs=[row, vec, vec, vec],
        out_specs=row,
        out_shape=jax.ShapeDtypeStruct((m, d), BF16),
        compiler_params=_params(("arbitrary",)),
        name="prenorm",
    )(x, g, sc, sh)


def _post_pre_kernel(x_ref, y_ref, gpost_ref, gate_ref, gpre_ref, sc_ref, sh_ref, x1_ref, h_ref):
    x1 = x_ref[...] + gate_ref[...] * _rms(y_ref[...].astype(F32), gpost_ref[...])
    x1_ref[...] = x1
    h = _rms(x1, gpre_ref[...]) * (1.0 + sc_ref[...]) + sh_ref[...]
    h_ref[...] = h.astype(h_ref.dtype)


def _post_pre(x, y, gpost, gate, gpre, sc, sh, tm=256):
    m, d = x.shape
    row = pl.BlockSpec((tm, d), lambda i: (i, 0))
    vec = pl.BlockSpec((1, d), lambda i: (0, 0))
    return pl.pallas_call(
        _post_pre_kernel,
        grid=(m // tm,),
        in_specs=[row, row, vec, vec, vec, vec, vec],
        out_specs=[row, row],
        out_shape=[jax.ShapeDtypeStruct((m, d), F32), jax.ShapeDtypeStruct((m, d), BF16)],
        compiler_params=_params(("arbitrary",)),
        name="post_pre",
    )(x, y, gpost, gate, gpre, sc, sh)


def _post_kernel(x_ref, y_ref, gpost_ref, gate_ref, o_ref):
    o_ref[...] = x_ref[...] + gate_ref[...] * _rms(y_ref[...].astype(F32), gpost_ref[...])


def _post(x, y, gpost, gate, tm=256):
    m, d = x.shape
    row = pl.BlockSpec((tm, d), lambda i: (i, 0))
    vec = pl.BlockSpec((1, d), lambda i: (0, 0))
    return pl.pallas_call(
        _post_kernel,
        grid=(m // tm,),
        in_specs=[row, row, vec, vec],
        out_specs=row,
        out_shape=jax.ShapeDtypeStruct((m, d), F32),
        compiler_params=_params(("arbitrary",)),
        name="post",
    )(x, y, gpost, gate)


def _gelu(p):
    return 0.5 * p * (1.0 + lax.erf(p * (1.0 / math.sqrt(2.0))))


def _sigmoid(p):
    return 0.5 * jnp.tanh(0.5 * p) + 0.5


def _mm_kernel(a_ref, w_ref, o_ref, *, epilogues):
    j = pl.program_id(1)

    def run(fn):
        p = jnp.dot(a_ref[...], w_ref[...].astype(BF16), preferred_element_type=F32)
        o_ref[...] = fn(p).astype(o_ref.dtype)

    if len(epilogues) == 1:
        run(epilogues[0][2])
        return
    for lo, hi, fn in epilogues:
        pl.when(jnp.logical_and(j >= lo, j < hi))(functools.partial(run, fn))


def _matmul(a, w, *, epilogues, out_dtype, tm, tn, name):
    m, k = a.shape
    n = w.shape[1]
    assert epilogues[0][0] == 0 and epilogues[-1][1] == n // tn
    return pl.pallas_call(
        functools.partial(_mm_kernel, epilogues=epilogues),
        grid=(m // tm, n // tn),
        in_specs=[pl.BlockSpec((tm, k), lambda i, j: (i, 0)),
                  pl.BlockSpec((k, tn), lambda i, j: (0, j))],
        out_specs=pl.BlockSpec((tm, tn), lambda i, j: (i, j)),
        out_shape=jax.ShapeDtypeStruct((m, n), out_dtype),
        compiler_params=_params(("arbitrary", "arbitrary")),
        name=name,
    )(a, w)


MMK_ROWS = 256


def _mmk_kernel(a_ref, w_ref, o_ref, acc_ref):
    k = pl.program_id(2)
    last = pl.num_programs(2) - 1

    def chunks(store):
        w = w_ref[...].astype(BF16)
        for r in range(a_ref.shape[0] // MMK_ROWS):
            rows = pl.ds(r * MMK_ROWS, MMK_ROWS)
            store(rows, jnp.dot(a_ref[rows, :], w, preferred_element_type=F32))

    def first(rows, p):
        acc_ref[rows, :] = p

    def middle(rows, p):
        acc_ref[rows, :] += p

    def final(rows, p):
        o_ref[rows, :] = (acc_ref[rows, :] + p).astype(o_ref.dtype)

    pl.when(k == 0)(functools.partial(chunks, first))
    pl.when(jnp.logical_and(k > 0, k < last))(functools.partial(chunks, middle))
    pl.when(k == last)(functools.partial(chunks, final))


def _matmul_ktiled(a, w, *, out_dtype, tm, tn, tk, name):
    m, k = a.shape
    n = w.shape[1]
    assert k // tk >= 2
    return pl.pallas_call(
        _mmk_kernel,
        grid=(m // tm, n // tn, k // tk),
        in_specs=[pl.BlockSpec((tm, tk), lambda i, j, kk: (i, kk)),
                  pl.BlockSpec((tk, tn), lambda i, j, kk: (kk, j))],
        out_specs=pl.BlockSpec((tm, tn), lambda i, j, kk: (i, j)),
        out_shape=jax.ShapeDtypeStruct((m, n), out_dtype),
        scratch_shapes=[pltpu.VMEM((tm, tn), F32)],
        compiler_params=_params(("arbitrary", "arbitrary", "arbitrary")),
        name=name,
    )(a, w)


def _merge_kernel(ga_ref, ob_ref, wa_ref, wb_ref, sa_ref, sb_ref, o_ref):
    ya = jnp.dot(ga_ref[...], wa_ref[...].astype(BF16), preferred_element_type=F32)
    yb = jnp.dot(ob_ref[...], wb_ref[...].astype(BF16), preferred_element_type=F32)
    mix = sa_ref[...].astype(F32) * ya + sb_ref[...].astype(F32) * yb
    o_ref[...] = mix.astype(o_ref.dtype)


def _merge(ga, ob, wa, wb, proj, gate_col, tm=1024, tn=512):
    m, k = ga.shape
    n = wa.shape[1]
    nb = n // tn
    ja = gate_col // tn
    act = pl.BlockSpec((tm, k), lambda i, j: (i, 0))
    wsp = pl.BlockSpec((k, tn), lambda i, j: (0, j))
    return pl.pallas_call(
        _merge_kernel,
        grid=(m // tm, nb),
        in_specs=[act, act, wsp, wsp,
                  pl.BlockSpec((tm, tn), lambda i, j: (i, j + ja)),
                  pl.BlockSpec((tm, tn), lambda i, j: (i, j + ja + nb))],
        out_specs=pl.BlockSpec((tm, tn), lambda i, j: (i, j)),
        out_shape=jax.ShapeDtypeStruct((m, n), BF16),
        compiler_params=_params(("arbitrary", "arbitrary")),
        name="merge",
    )(ga, ob, wa, wb, proj, proj)


def _gmlp_kernel(u_ref, v_ref, gv_ref, ws_ref, bs_ref, o_ref):
    rows = u_ref.shape[0]
    v = v_ref[...].astype(F32)
    mu = jnp.mean(v, axis=-1, keepdims=True)
    vc = v - mu
    inv = lax.rsqrt(jnp.mean(vc * vc, axis=-1, keepdims=True) + EPS)
    vn = (vc * inv * gv_ref[...]).astype(BF16)
    t = lax.broadcasted_iota(jnp.int32, (GMLP_BLOCK, GMLP_BLOCK), 0)
    s = lax.broadcasted_iota(jnp.int32, (GMLP_BLOCK, GMLP_BLOCK), 1)
    chunk_causal = (t // CHUNK) >= (s // CHUNK)
    gd = GMLP_WIDTH // GMLP_GROUPS
    for g in range(GMLP_GROUPS):
        w = jnp.where(chunk_causal, ws_ref[g], 0.0).astype(BF16)
        bias = bs_ref[g]
        for b in range(rows // GMLP_BLOCK):
            r = slice(b * GMLP_BLOCK, (b + 1) * GMLP_BLOCK)
            c = slice(g * gd, (g + 1) * gd)
            y = jnp.dot(w, vn[r, c], preferred_element_type=F32) + bias
            o_ref[r, c] = (u_ref[r, c].astype(F32) * y).astype(o_ref.dtype)


def _gmlp(zuv, g_v, w_s, b_s, rows=256):
    m = zuv.shape[0]
    half = pl.BlockSpec((rows, GMLP_WIDTH), lambda i: (i, 0))
    return pl.pallas_call(
        _gmlp_kernel,
        grid=(m // rows,),
        in_specs=[half,
                  pl.BlockSpec((rows, GMLP_WIDTH), lambda i: (i, 1)),
                  pl.BlockSpec((1, GMLP_WIDTH), lambda i: (0, 0)),
                  pl.BlockSpec((GMLP_GROUPS, GMLP_BLOCK, GMLP_BLOCK), lambda i: (0, 0, 0)),
                  pl.BlockSpec((GMLP_GROUPS, GMLP_BLOCK, 1), lambda i: (0, 0, 0))],
        out_specs=half,
        out_shape=jax.ShapeDtypeStruct((m, GMLP_WIDTH), BF16),
        compiler_params=_params(("arbitrary",)),
        name="gmlp",
    )(zuv, zuv, g_v, w_s, b_s[:, :, None])


def _sb_block(q, kj, vj, carry, upper2, mask):
    z = lax.dot_general(q, kj, (((1,), (1,)), ((), ())), preferred_element_type=F32)
    e = jnp.exp2(-jnp.abs(z))
    log_beta = jnp.minimum(z, 0.0) - jnp.log(1.0 + e) * LOG2_E
    lom = log_beta - z
    if mask is not None:
        lom = jnp.where(mask, lom, 0.0)
    hi = lom.astype(BF16)
    lo = (lom - hi.astype(F32)).astype(BF16)
    between = jnp.dot(jnp.concatenate([hi, lo], axis=1), upper2, preferred_element_type=F32)
    a = jnp.exp2(log_beta + between + carry)
    if mask is not None:
        a = jnp.where(mask, a, 0.0)
    pv = jnp.dot(a.astype(BF16), vj, preferred_element_type=F32)
    return pv, jnp.sum(lom, axis=1, keepdims=True)


def _attn_kernel(q_ref, k_ref, v_ref, o_ref, acc_ref, carry_ref):
    i = pl.program_id(1)
    row = lax.broadcasted_iota(jnp.int32, (ATT_TQ, ATT_TK), 0)
    col = lax.broadcasted_iota(jnp.int32, (ATT_TQ, ATT_TK), 1)
    upper = (row > col).astype(BF16)
    upper2 = jnp.concatenate([upper, upper], axis=0)
    causal = col < row
    heads = [slice(a * SB_HEAD_DIM, (a + 1) * SB_HEAD_DIM) for a in range(ATT_HEADS_PER_STEP)]

    def tile(a, j, carry, mask):
        off = pl.multiple_of(j * ATT_TK, ATT_TK)
        return _sb_block(q_ref[:, heads[a]], k_ref[pl.ds(off, ATT_TK), heads[a]],
                         v_ref[pl.ds(off, ATT_TK), heads[a]], carry, upper2, mask)

    no_carry = jnp.zeros((ATT_TQ, 1), F32)

    @pl.when(i == 0)
    def _():
        for a in range(ATT_HEADS_PER_STEP):
            pv, _ = tile(a, i, no_carry, causal)
            o_ref[:, heads[a]] = pv.astype(o_ref.dtype)

    @pl.when(i > 0)
    def _():
        for a in range(ATT_HEADS_PER_STEP):
            pv0, tot0 = tile(a, i, no_carry, causal)
            pv1, tot1 = tile(a, i - 1, tot0, None)
            acc_ref[a] = pv0 + pv1
            carry_ref[a] = tot0 + tot1

        def more(state):
            j, max_carry = state
            return jnp.logical_and(j >= 0, max_carry > -EXP2_UNDERFLOW)

        def sweep(state):
            j, _ = state
            for a in range(ATT_HEADS_PER_STEP):
                pv, tot = tile(a, j, carry_ref[a], None)
                acc_ref[a] += pv
                carry_ref[a] += tot
            return j - 1, jnp.max(carry_ref[...])

        lax.while_loop(more, sweep, (i - 2, jnp.max(carry_ref[...])))
        for a in range(ATT_HEADS_PER_STEP):
            o_ref[:, heads[a]] = acc_ref[a].astype(o_ref.dtype)


def _attention(qkv, q_col):
    s = qkv.shape[0]
    assert ATT_TQ == ATT_TK and s % ATT_TQ == 0
    width = ATT_HEADS_PER_STEP * SB_HEAD_DIM
    groups = SB_HEADS // ATT_HEADS_PER_STEP
    g0 = q_col // width
    return pl.pallas_call(
        _attn_kernel,
        grid=(groups, s // ATT_TQ),
        in_specs=[pl.BlockSpec((ATT_TQ, width), lambda h, i: (i, g0 + h)),
                  pl.BlockSpec((s, width), lambda h, i: (0, g0 + groups + h)),
                  pl.BlockSpec((s, width), lambda h, i: (0, g0 + 2 * groups + h))],
        out_specs=pl.BlockSpec((ATT_TQ, width), lambda h, i: (i, h)),
        out_shape=jax.ShapeDtypeStruct((s, SB_WIDTH), BF16),
        scratch_shapes=[pltpu.VMEM((ATT_HEADS_PER_STEP, ATT_TQ, SB_HEAD_DIM), F32),
                        pltpu.VMEM((ATT_HEADS_PER_STEP, ATT_TQ, 1), F32)],
        compiler_params=_params(("arbitrary", "arbitrary")),
        name="sb_attention",
    )(qkv, qkv, qkv)


def kernel(x, c, w_ada, b_ada, g_pre_mix, w_in, g_v, w_s, b_s, w_proj_a, w_proj_b, w_o,
           g_post_mix, g_pre_mlp, w_ff1, w_ff2, g_post_mlp):
    batch, seq, d = x.shape
    assert batch == 1 and d == D_MODEL
    depth = w_ada.shape[0]
    xs = x.reshape(seq, d)
    q_scale = LOG2_E / math.sqrt(SB_HEAD_DIM)
    tn_in = 512
    q_col = 2 * GMLP_WIDTH
    gate_col = q_col + 3 * SB_WIDTH
    t_q, t_k, t_gate, t_end = [c // tn_in for c in (q_col, q_col + SB_WIDTH, gate_col, gate_col + 2 * d)]
    in_epilogues = ((0, t_q, _gelu),
                    (t_q, t_k, lambda p: p * q_scale),
                    (t_k, t_gate, lambda p: p),
                    (t_gate, t_end, _sigmoid))

    for l in range(depth):
        mod = _adaln(c, w_ada[l], b_ada[l][None, :])
        sh1, sc1, gt1, sh2, sc2, gt2 = [mod[:, n * d:(n + 1) * d] for n in range(N_MOD)]
        row = lambda v: v[None, :]

        h = _prenorm(xs, row(g_pre_mix[l]), sc1, sh1)
        proj = _matmul(h, w_in[l], epilogues=in_epilogues, out_dtype=BF16, tm=1024, tn=tn_in, name="in_proj")
        ga = _gmlp(proj, row(g_v[l]), w_s[l], b_s[l])
        ob = _attention(proj, q_col)
        mixin = _merge(ga, ob, w_proj_a[l], w_proj_b[l], proj, gate_col)
        mix = _matmul(mixin, w_o[l], epilogues=((0, d // 512, lambda p: p),),
                      out_dtype=BF16, tm=1024, tn=512, name="w_o")
        xs, h2 = _post_pre(xs, mix, row(g_post_mix[l]), gt1, row(g_pre_mlp[l]), sc2, sh2)

        hid = _matmul(h2, w_ff1[l], epilogues=((0, D_FF // 512, lambda p: jnp.square(jnp.maximum(p, 0.0))),),
                      out_dtype=BF16, tm=1024, tn=512, name="ff1")
        ff = _matmul_ktiled(hid, w_ff2[l], out_dtype=BF16, tm=2048, tn=1024, tk=2048, name="ff2")
        xs = _post(xs, ff, row(g_post_mlp[l]), gt2)
    return xs.reshape(batch, seq, d)
```

```python
import functools
import math

import jax
import jax.numpy as jnp
from jax import lax
from jax.experimental import pallas as pl
from jax.experimental.pallas import tpu as pltpu

F32 = jnp.float32
BF16 = jnp.bfloat16

D_MODEL = 4096
CHUNK = 64
GMLP_BLOCK = 128
GMLP_GROUPS = 16
GMLP_WIDTH = D_MODEL // 2
SB_HEAD_DIM = 128
SB_HEADS = (D_MODEL // 2) // SB_HEAD_DIM
SB_WIDTH = SB_HEADS * SB_HEAD_DIM
D_FF = 4 * D_MODEL
N_MOD = 6
EPS = 1e-6
LANES = 128
MIB = 1024 * 1024
VMEM_LIMIT = 56 * MIB

ATT_TQ = 256
ATT_TK = 256
ATT_HEADS_PER_STEP = 4
LOG2_E = 1.0 / math.log(2.0)
EXP2_UNDERFLOW = 160.0


def _params(semantics, vmem=VMEM_LIMIT):
    return pltpu.CompilerParams(dimension_semantics=semantics, vmem_limit_bytes=vmem)


def _ada_kernel(c_ref, w_ref, b_ref, o_ref):
    c = c_ref[...]
    s = c * jax.nn.sigmoid(c)
    s8 = jnp.broadcast_to(s, (8, s.shape[1])).astype(BF16)
    p = jnp.dot(s8, w_ref[...].astype(BF16), preferred_element_type=F32)
    o_ref[...] = p[0:1, :] + b_ref[...]


def _adaln(c, w, b, tn=1024):
    d, n = w.shape
    return pl.pallas_call(
        _ada_kernel,
        grid=(n // tn,),
        in_specs=[pl.BlockSpec((1, d), lambda j: (0, 0)),
                  pl.BlockSpec((d, tn), lambda j: (0, j)),
                  pl.BlockSpec((1, tn), lambda j: (0, j))],
        out_specs=pl.BlockSpec((1, tn), lambda j: (0, j)),
        out_shape=jax.ShapeDtypeStruct((1, n), F32),
        compiler_params=_params(("arbitrary",)),
        name="adaln",
    )(c, w, b)


def _rms(x, g):
    inv = lax.rsqrt(jnp.mean(x * x, axis=-1, keepdims=True) + EPS)
    return x * inv * g


def _prenorm_kernel(x_ref, g_ref, sc_ref, sh_ref, o_ref):
    h = _rms(x_ref[...], g_ref[...]) * (1.0 + sc_ref[...]) + sh_ref[...]
    o_ref[...] = h.astype(o_ref.dtype)


def _prenorm(x, g, sc, sh, tm=512):
    m, d = x.shape
    row = pl.BlockSpec((tm, d), lambda i: (i, 0))
    vec = pl.BlockSpec((1, d), lambda i: (0, 0))
    return pl.pallas_call(
        _prenorm_kernel,
        grid=(m // tm,),
        in_specs=[row, vec, vec, vec],
        out_specs=row,
        out_shape=jax.ShapeDtypeStruct((m, d), BF16),
        compiler_params=_params(("arbitrary",)),
        name="prenorm",
    )(x, g, sc, sh)


def _post_pre_kernel(x_ref, y_ref, gpost_ref, gate_ref, gpre_ref, sc_ref, sh_ref, x1_ref, h_ref):
    x1 = x_ref[...] + gate_ref[...] * _rms(y_ref[...].astype(F32), gpost_ref[...])
    x1_ref[...] = x1
    h = _rms(x1, gpre_ref[...]) * (1.0 + sc_ref[...]) + sh_ref[...]
    h_ref[...] = h.astype(h_ref.dtype)


def _post_pre(x, y, gpost, gate, gpre, sc, sh, tm=256):
    m, d = x.shape
    row = pl.BlockSpec((tm, d), lambda i: (i, 0))
    vec = pl.BlockSpec((1, d), lambda i: (0, 0))
    return pl.pallas_call(
        _post_pre_kernel,
        grid=(m // tm,),
        in_specs=[row, row, vec, vec, vec, vec, vec],
        out_specs=[row, row],
        out_shape=[jax.ShapeDtypeStruct((m, d), F32), jax.ShapeDtypeStruct((m, d), BF16)],
        compiler_params=_params(("arbitrary",)),
        name="post_pre",
    )(x, y, gpost, gate, gpre, sc, sh)


def _post_kernel(x_ref, y_ref, gpost_ref, gate_ref, o_ref):
    o_ref[...] = x_ref[...] + gate_ref[...] * _rms(y_ref[...].astype(F32), gpost_ref[...])


def _post(x, y, gpost, gate, tm=256):
    m, d = x.shape
    row = pl.BlockSpec((tm, d), lambda i: (i, 0))
    vec = pl.BlockSpec((1, d), lambda i: (0, 0))
    return pl.pallas_call(
        _post_kernel,
        grid=(m // tm,),
        in_specs=[row, row, vec, vec],
        out_specs=row,
        out_shape=jax.ShapeDtypeStruct((m, d), F32),
        compiler_params=_params(("arbitrary",)),
        name="post",
    )(x, y, gpost, gate)


def _gelu(p):
    return 0.5 * p * (1.0 + lax.erf(p * (1.0 / math.sqrt(2.0))))


def _sigmoid(p):
    return 0.5 * jnp.tanh(0.5 * p) + 0.5


def _mm_kernel(a_ref, w_ref, *refs, epilogues, n_casts):
    cast_src = refs[:n_casts]
    o_ref = refs[n_casts]
    cast_dst = refs[n_casts + 1:]
    j = pl.program_id(1)

    def run(fn):
        p = jnp.dot(a_ref[...], w_ref[...].astype(BF16), preferred_element_type=F32)
        o_ref[...] = fn(p).astype(o_ref.dtype)
        for src, dst in zip(cast_src, cast_dst):
            dst[...] = src[...].astype(dst.dtype)

    if len(epilogues) == 1:
        run(epilogues[0][2])
        return
    for lo, hi, fn in epilogues:
        pl.when(jnp.logical_and(j >= lo, j < hi))(functools.partial(run, fn))


BF16_ROWS = 16


def _matmul(a, w, *, epilogues, out_dtype, tm, tn, name, casts=()):
    m, k = a.shape
    n = w.shape[1]
    ni, nj = m // tm, n // tn
    assert epilogues[0][0] == 0 and epilogues[-1][1] == nj
    cast_specs, cast_shapes = [], []
    for cw in casts:
        r, c = cw.shape
        rows = BF16_ROWS * pl.cdiv(r, BF16_ROWS * ni * nj)
        assert r % rows == 0
        last = r // rows - 1
        cast_specs.append(pl.BlockSpec((rows, c), lambda i, j, last=last: (jnp.minimum(i * nj + j, last), 0)))
        cast_shapes.append(jax.ShapeDtypeStruct((r, c), BF16))
    res = pl.pallas_call(
        functools.partial(_mm_kernel, epilogues=epilogues, n_casts=len(casts)),
        grid=(ni, nj),
        in_specs=[pl.BlockSpec((tm, k), lambda i, j: (i, 0)),
                  pl.BlockSpec((k, tn), lambda i, j: (0, j))] + cast_specs,
        out_specs=[pl.BlockSpec((tm, tn), lambda i, j: (i, j))] + cast_specs,
        out_shape=[jax.ShapeDtypeStruct((m, n), out_dtype)] + cast_shapes,
        compiler_params=_params(("arbitrary", "arbitrary")),
        name=name,
    )(a, w, *casts)
    return res if casts else res[0]


MMK_ROWS = 256


def _mmk_kernel(a_ref, w_ref, o_ref, acc_ref):
    k = pl.program_id(2)
    last = pl.num_programs(2) - 1

    def chunks(store):
        w = w_ref[...].astype(BF16)
        for r in range(a_ref.shape[0] // MMK_ROWS):
            rows = pl.ds(r * MMK_ROWS, MMK_ROWS)
            store(rows, jnp.dot(a_ref[rows, :], w, preferred_element_type=F32))

    def first(rows, p):
        acc_ref[rows, :] = p

    def middle(rows, p):
        acc_ref[rows, :] += p

    def final(rows, p):
        o_ref[rows, :] = (acc_ref[rows, :] + p).astype(o_ref.dtype)

    pl.when(k == 0)(functools.partial(chunks, first))
    pl.when(jnp.logical_and(k > 0, k < last))(functools.partial(chunks, middle))
    pl.when(k == last)(functools.partial(chunks, final))


def _matmul_ktiled(a, w, *, out_dtype, tm, tn, tk, name):
    m, k = a.shape
    n = w.shape[1]
    assert k // tk >= 2
    return pl.pallas_call(
        _mmk_kernel,
        grid=(m // tm, n // tn, k // tk),
        in_specs=[pl.BlockSpec((tm, tk), lambda i, j, kk: (i, kk)),
                  pl.BlockSpec((tk, tn), lambda i, j, kk: (kk, j))],
        out_specs=pl.BlockSpec((tm, tn), lambda i, j, kk: (i, j)),
        out_shape=jax.ShapeDtypeStruct((m, n), out_dtype),
        scratch_shapes=[pltpu.VMEM((tm, tn), F32)],
        compiler_params=_params(("arbitrary", "arbitrary", "arbitrary")),
        name=name,
    )(a, w)


def _merge_kernel(ga_ref, ob_ref, wa_ref, wb_ref, sa_ref, sb_ref, o_ref):
    ya = jnp.dot(ga_ref[...], wa_ref[...].astype(BF16), preferred_element_type=F32)
    yb = jnp.dot(ob_ref[...], wb_ref[...].astype(BF16), preferred_element_type=F32)
    mix = sa_ref[...].astype(F32) * ya + sb_ref[...].astype(F32) * yb
    o_ref[...] = mix.astype(o_ref.dtype)


def _merge(ga, ob, wa, wb, proj, gate_col, tm=1024, tn=1024):
    m, k = ga.shape
    n = wa.shape[1]
    nb = n // tn
    ja = gate_col // tn
    act = pl.BlockSpec((tm, k), lambda i, j: (i, 0))
    wsp = pl.BlockSpec((k, tn), lambda i, j: (0, j))
    return pl.pallas_call(
        _merge_kernel,
        grid=(m // tm, nb),
        in_specs=[act, act, wsp, wsp,
                  pl.BlockSpec((tm, tn), lambda i, j: (i, j + ja)),
                  pl.BlockSpec((tm, tn), lambda i, j: (i, j + ja + nb))],
        out_specs=pl.BlockSpec((tm, tn), lambda i, j: (i, j)),
        out_shape=jax.ShapeDtypeStruct((m, n), BF16),
        compiler_params=_params(("arbitrary", "arbitrary")),
        name="merge",
    )(ga, ob, wa, wb, proj, proj)


def _gmlp_kernel(u_ref, v_ref, gv_ref, ws_ref, bs_ref, o_ref):
    rows = u_ref.shape[0]
    v = v_ref[...].astype(F32)
    mu = jnp.mean(v, axis=-1, keepdims=True)
    vc = v - mu
    inv = lax.rsqrt(jnp.mean(vc * vc, axis=-1, keepdims=True) + EPS)
    vn = (vc * inv * gv_ref[...]).astype(BF16)
    t = lax.broadcasted_iota(jnp.int32, (GMLP_BLOCK, GMLP_BLOCK), 0)
    s = lax.broadcasted_iota(jnp.int32, (GMLP_BLOCK, GMLP_BLOCK), 1)
    chunk_causal = (t // CHUNK) >= (s // CHUNK)
    gd = GMLP_WIDTH // GMLP_GROUPS
    for g in range(GMLP_GROUPS):
        w = jnp.where(chunk_causal, ws_ref[g], 0.0).astype(BF16)
        bias = bs_ref[g]
        for b in range(rows // GMLP_BLOCK):
            r = slice(b * GMLP_BLOCK, (b + 1) * GMLP_BLOCK)
            c = slice(g * gd, (g + 1) * gd)
            y = jnp.dot(w, vn[r, c], preferred_element_type=F32) + bias
            o_ref[r, c] = (u_ref[r, c].astype(F32) * y).astype(o_ref.dtype)


def _gmlp(zuv, g_v, w_s, b_s, rows=256):
    m = zuv.shape[0]
    half = pl.BlockSpec((rows, GMLP_WIDTH), lambda i: (i, 0))
    return pl.pallas_call(
        _gmlp_kernel,
        grid=(m // rows,),
        in_specs=[half,
                  pl.BlockSpec((rows, GMLP_WIDTH), lambda i: (i, 1)),
                  pl.BlockSpec((1, GMLP_WIDTH), lambda i: (0, 0)),
                  pl.BlockSpec((GMLP_GROUPS, GMLP_BLOCK, GMLP_BLOCK), lambda i: (0, 0, 0)),
                  pl.BlockSpec((GMLP_GROUPS, GMLP_BLOCK, 1), lambda i: (0, 0, 0))],
        out_specs=half,
        out_shape=jax.ShapeDtypeStruct((m, GMLP_WIDTH), BF16),
        compiler_params=_params(("arbitrary",)),
        name="gmlp",
    )(zuv, zuv, g_v, w_s, b_s[:, :, None])


def _sb_block(q, kj, vj, carry, upper2, mask):
    z = lax.dot_general(q, kj, (((1,), (1,)), ((), ())), preferred_element_type=F32)
    e = jnp.exp2(-jnp.abs(z))
    log_beta = jnp.minimum(z, 0.0) - jnp.log(1.0 + e) * LOG2_E
    lom = log_beta - z
    if mask is not None:
        lom = jnp.where(mask, lom, 0.0)
    hi = lom.astype(BF16)
    lo = (lom - hi.astype(F32)).astype(BF16)
    between = jnp.dot(jnp.concatenate([hi, lo], axis=1), upper2, preferred_element_type=F32)
    a = jnp.exp2(log_beta + between + carry)
    if mask is not None:
        a = jnp.where(mask, a, 0.0)
    pv = jnp.dot(a.astype(BF16), vj, preferred_element_type=F32)
    return pv, jnp.sum(lom, axis=1, keepdims=True)


def _attn_kernel(q_ref, k_ref, v_ref, o_ref, acc_ref, carry_ref):
    i = pl.program_id(1)
    row = lax.broadcasted_iota(jnp.int32, (ATT_TQ, ATT_TK), 0)
    col = lax.broadcasted_iota(jnp.int32, (ATT_TQ, ATT_TK), 1)
    upper = (row > col).astype(BF16)
    upper2 = jnp.concatenate([upper, upper], axis=0)
    causal = col < row
    heads = [slice(a * SB_HEAD_DIM, (a + 1) * SB_HEAD_DIM) for a in range(ATT_HEADS_PER_STEP)]

    def tile(a, j, carry, mask):
        off = pl.multiple_of(j * ATT_TK, ATT_TK)
        return _sb_block(q_ref[:, heads[a]], k_ref[pl.ds(off, ATT_TK), heads[a]],
                         v_ref[pl.ds(off, ATT_TK), heads[a]], carry, upper2, mask)

    no_carry = jnp.zeros((ATT_TQ, 1), F32)

    @pl.when(i == 0)
    def _():
        for a in range(ATT_HEADS_PER_STEP):
            pv, _ = tile(a, i, no_carry, causal)
            o_ref[:, heads[a]] = pv.astype(o_ref.dtype)

    @pl.when(i > 0)
    def _():
        for a in range(ATT_HEADS_PER_STEP):
            pv0, tot0 = tile(a, i, no_carry, causal)
            pv1, tot1 = tile(a, i - 1, tot0, None)
            acc_ref[a] = pv0 + pv1
            carry_ref[a] = tot0 + tot1

        def more(state):
            j, max_carry = state
            return jnp.logical_and(j >= 0, max_carry > -EXP2_UNDERFLOW)

        def sweep(state):
            j, _ = state
            for a in range(ATT_HEADS_PER_STEP):
                pv, tot = tile(a, j, carry_ref[a], None)
                acc_ref[a] += pv
                carry_ref[a] += tot
            return j - 1, jnp.max(carry_ref[...])

        lax.while_loop(more, sweep, (i - 2, jnp.max(carry_ref[...])))
        for a in range(ATT_HEADS_PER_STEP):
            o_ref[:, heads[a]] = acc_ref[a].astype(o_ref.dtype)


def _attention(qkv, q_col):
    s = qkv.shape[0]
    assert ATT_TQ == ATT_TK and s % ATT_TQ == 0
    width = ATT_HEADS_PER_STEP * SB_HEAD_DIM
    groups = SB_HEADS // ATT_HEADS_PER_STEP
    g0 = q_col // width
    return pl.pallas_call(
        _attn_kernel,
        grid=(groups, s // ATT_TQ),
        in_specs=[pl.BlockSpec((ATT_TQ, width), lambda h, i: (i, g0 + h)),
                  pl.BlockSpec((s, width), lambda h, i: (0, g0 + groups + h)),
                  pl.BlockSpec((s, width), lambda h, i: (0, g0 + 2 * groups + h))],
        out_specs=pl.BlockSpec((ATT_TQ, width), lambda h, i: (i, h)),
        out_shape=jax.ShapeDtypeStruct((s, SB_WIDTH), BF16),
        scratch_shapes=[pltpu.VMEM((ATT_HEADS_PER_STEP, ATT_TQ, SB_HEAD_DIM), F32),
                        pltpu.VMEM((ATT_HEADS_PER_STEP, ATT_TQ, 1), F32)],
        compiler_params=_params(("arbitrary", "arbitrary")),
        name="sb_attention",
    )(qkv, qkv, qkv)


def kernel(x, c, w_ada, b_ada, g_pre_mix, w_in, g_v, w_s, b_s, w_proj_a, w_proj_b, w_o,
           g_post_mix, g_pre_mlp, w_ff1, w_ff2, g_post_mlp):
    batch, seq, d = x.shape
    assert batch == 1 and d == D_MODEL
    depth = w_ada.shape[0]
    xs = x.reshape(seq, d)
    q_scale = LOG2_E / math.sqrt(SB_HEAD_DIM)
    tn_in = 512
    q_col = 2 * GMLP_WIDTH
    gate_col = q_col + 3 * SB_WIDTH
    t_q, t_k, t_gate, t_end = [c // tn_in for c in (q_col, q_col + SB_WIDTH, gate_col, gate_col + 2 * d)]
    in_epilogues = ((0, t_q, _gelu),
                    (t_q, t_k, lambda p: p * q_scale),
                    (t_k, t_gate, lambda p: p),
                    (t_gate, t_end, _sigmoid))

    for l in range(depth):
        mod = _adaln(c, w_ada[l], b_ada[l][None, :])
        sh1, sc1, gt1, sh2, sc2, gt2 = [mod[:, n * d:(n + 1) * d] for n in range(N_MOD)]
        row = lambda v: v[None, :]

        h = _prenorm(xs, row(g_pre_mix[l]), sc1, sh1)
        proj, wa, wb, wo, wf1 = _matmul(
            h, w_in[l], epilogues=in_epilogues, out_dtype=BF16, tm=1024, tn=tn_in, name="in_proj",
            casts=(w_proj_a[l], w_proj_b[l], w_o[l], w_ff1[l]))
        ga = _gmlp(proj, row(g_v[l]), w_s[l], b_s[l])
        ob = _attention(proj, q_col)
        mixin = _merge(ga, ob, wa, wb, proj, gate_col)
        mix = _matmul(mixin, wo, epilogues=((0, d // 1024, lambda p: p),),
                      out_dtype=BF16, tm=1024, tn=1024, name="w_o")
        xs, h2 = _post_pre(xs, mix, row(g_post_mix[l]), gt1, row(g_pre_mlp[l]), sc2, sh2)

        hid, wf2 = _matmul(h2, wf1, epilogues=((0, D_FF // 1024, lambda p: jnp.square(jnp.maximum(p, 0.0))),),
                           out_dtype=BF16, tm=1024, tn=1024, name="ff1", casts=(w_ff2[l],))
        ff = _matmul_ktiled(hid, wf2, out_dtype=BF16, tm=2048, tn=1024, tk=2048, name="ff2")
        xs = _post(xs, ff, row(g_post_mlp[l]), gt2)
    return xs.reshape(batch, seq, d)
```

```python
import functools
import math

import jax
import jax.numpy as jnp
from jax import lax
from jax.experimental import pallas as pl
from jax.experimental.pallas import tpu as pltpu

F32 = jnp.float32
BF16 = jnp.bfloat16

D_MODEL = 4096
CHUNK = 64
GMLP_BLOCK = 128
GMLP_GROUPS = 16
GMLP_WIDTH = D_MODEL // 2
SB_HEAD_DIM = 128
SB_HEADS = (D_MODEL // 2) // SB_HEAD_DIM
SB_WIDTH = SB_HEADS * SB_HEAD_DIM
D_FF = 4 * D_MODEL
N_MOD = 6
EPS = 1e-6
LANES = 128
MIB = 1024 * 1024
VMEM_LIMIT = 56 * MIB

ATT_TQ = 256
ATT_TK = 256
ATT_HEADS_PER_STEP = 4
LOG2_E = 1.0 / math.log(2.0)
EXP2_UNDERFLOW = 160.0


def _params(semantics, vmem=VMEM_LIMIT):
    return pltpu.CompilerParams(dimension_semantics=semantics, vmem_limit_bytes=vmem)


def _ada_kernel(c_ref, w_ref, b_ref, o_ref):
    c = c_ref[...]
    s = c * jax.nn.sigmoid(c)
    s8 = jnp.broadcast_to(s, (8, s.shape[1])).astype(BF16)
    p = jnp.dot(s8, w_ref[...].astype(BF16), preferred_element_type=F32)
    o_ref[...] = p[0:1, :] + b_ref[...]


def _adaln(c, w, b, n_out, tn=1024):
    d = w.shape[0]
    return pl.pallas_call(
        _ada_kernel,
        grid=(n_out // tn,),
        in_specs=[pl.BlockSpec((1, d), lambda j: (0, 0)),
                  pl.BlockSpec((d, tn), lambda j: (0, j)),
                  pl.BlockSpec((1, tn), lambda j: (0, j))],
        out_specs=pl.BlockSpec((1, tn), lambda j: (0, j)),
        out_shape=jax.ShapeDtypeStruct((1, n_out), F32),
        compiler_params=_params(("arbitrary",)),
        name="adaln",
    )(c, w, b)


def _rms(x, g):
    inv = lax.rsqrt(jnp.mean(x * x, axis=-1, keepdims=True) + EPS)
    return x * inv * g


def _prenorm_kernel(x_ref, g_ref, sc_ref, sh_ref, o_ref):
    h = _rms(x_ref[...], g_ref[...]) * (1.0 + sc_ref[...]) + sh_ref[...]
    o_ref[...] = h.astype(o_ref.dtype)


def _prenorm(x, g, sc, sh, tm=512):
    m, d = x.shape
    row = pl.BlockSpec((tm, d), lambda i: (i, 0))
    vec = pl.BlockSpec((1, d), lambda i: (0, 0))
    return pl.pallas_call(
        _prenorm_kernel,
        grid=(m // tm,),
        in_specs=[row, vec, vec, vec],
        out_specs=row,
        out_shape=jax.ShapeDtypeStruct((m, d), BF16),
        compiler_params=_params(("arbitrary",)),
        name="prenorm",
    )(x, g, sc, sh)


def _post_pre_kernel(x_ref, y_ref, gpost_ref, gate_ref, gpre_ref, sc_ref, sh_ref, x1_ref, h_ref):
    x1 = x_ref[...] + gate_ref[...] * _rms(y_ref[...].astype(F32), gpost_ref[...])
    x1_ref[...] = x1
    h = _rms(x1, gpre_ref[...]) * (1.0 + sc_ref[...]) + sh_ref[...]
    h_ref[...] = h.astype(h_ref.dtype)


def _post_pre(x, y, gpost, gate, gpre, sc, sh, tm=256):
    m, d = x.shape
    row = pl.BlockSpec((tm, d), lambda i: (i, 0))
    vec = pl.BlockSpec((1, d), lambda i: (0, 0))
    return pl.pallas_call(
        _post_pre_kernel,
        grid=(m // tm,),
        in_specs=[row, row, vec, vec, vec, vec, vec],
        out_specs=[row, row],
        out_shape=[jax.ShapeDtypeStruct((m, d), F32), jax.ShapeDtypeStruct((m, d), BF16)],
        compiler_params=_params(("arbitrary",)),
        name="post_pre",
    )(x, y, gpost, gate, gpre, sc, sh)


def _post_kernel(x_ref, y_ref, gpost_ref, gate_ref, o_ref):
    o_ref[...] = x_ref[...] + gate_ref[...] * _rms(y_ref[...].astype(F32), gpost_ref[...])


def _post(x, y, gpost, gate, tm=256):
    m, d = x.shape
    row = pl.BlockSpec((tm, d), lambda i: (i, 0))
    vec = pl.BlockSpec((1, d), lambda i: (0, 0))
    return pl.pallas_call(
        _post_kernel,
        grid=(m // tm,),
        in_specs=[row, row, vec, vec],
        out_specs=row,
        out_shape=jax.ShapeDtypeStruct((m, d), F32),
        compiler_params=_params(("arbitrary",)),
        name="post",
    )(x, y, gpost, gate)


def _gelu(p):
    return 0.5 * p * (1.0 + lax.erf(p * (1.0 / math.sqrt(2.0))))


def _sigmoid(p):
    return 0.5 * jnp.tanh(0.5 * p) + 0.5


def _mm_kernel(a_hbm, w_ref, *refs, epilogues, n_casts):
    cast_src = refs[:n_casts]
    o_ref = refs[n_casts]
    cast_dst = refs[n_casts + 1:2 * n_casts + 1]
    a_buf, a_sem = refs[2 * n_casts + 1:]
    tm = a_buf.shape[1]
    i = pl.program_id(0)
    j = pl.program_id(1)
    slot = i % 2

    def a_copy(tile, s):
        return pltpu.make_async_copy(a_hbm.at[pl.ds(tile * tm, tm), :], a_buf.at[s], a_sem.at[s])

    @pl.when(j == 0)
    def _():
        @pl.when(i == 0)
        def _():
            a_copy(0, 0).start()

        a_copy(i, slot).wait()

        @pl.when(i + 1 < pl.num_programs(0))
        def _():
            a_copy(i + 1, 1 - slot).start()

    def run(fn):
        p = jnp.dot(a_buf[slot], w_ref[...].astype(BF16), preferred_element_type=F32)
        o_ref[...] = fn(p).astype(o_ref.dtype)
        for src, dst in zip(cast_src, cast_dst):
            dst[...] = src[...].astype(dst.dtype)

    if len(epilogues) == 1:
        run(epilogues[0][2])
        return
    for lo, hi, fn in epilogues:
        pl.when(jnp.logical_and(j >= lo, j < hi))(functools.partial(run, fn))


BF16_ROWS = 16


def _matmul(a, w, *, epilogues, out_dtype, tm, tn, name, casts=()):
    m, k = a.shape
    n = w.shape[1]
    ni, nj = m // tm, n // tn
    assert epilogues[0][0] == 0 and epilogues[-1][1] == nj
    cast_specs, cast_shapes = [], []
    for cw in casts:
        r, c = cw.shape
        rows = BF16_ROWS * pl.cdiv(r, BF16_ROWS * ni * nj)
        assert r % rows == 0
        last = r // rows - 1
        cast_specs.append(pl.BlockSpec((rows, c), lambda i, j, last=last: (jnp.minimum(i * nj + j, last), 0)))
        cast_shapes.append(jax.ShapeDtypeStruct((r, c), BF16))
    res = pl.pallas_call(
        functools.partial(_mm_kernel, epilogues=epilogues, n_casts=len(casts)),
        grid=(ni, nj),
        in_specs=[pl.BlockSpec(memory_space=pl.ANY),
                  pl.BlockSpec((k, tn), lambda i, j: (0, j))] + cast_specs,
        out_specs=[pl.BlockSpec((tm, tn), lambda i, j: (i, j))] + cast_specs,
        out_shape=[jax.ShapeDtypeStruct((m, n), out_dtype)] + cast_shapes,
        scratch_shapes=[pltpu.VMEM((2, tm, k), a.dtype), pltpu.SemaphoreType.DMA((2,))],
        compiler_params=_params(("arbitrary", "arbitrary")),
        name=name,
    )(a, w, *casts)
    return res if casts else res[0]


MMK_ROWS = 256


def _mmk_kernel(a_ref, w_ref, o_ref, acc_ref):
    k = pl.program_id(2)
    last = pl.num_programs(2) - 1

    def chunks(store):
        w = w_ref[...].astype(BF16)
        for r in range(a_ref.shape[0] // MMK_ROWS):
            rows = pl.ds(r * MMK_ROWS, MMK_ROWS)
            store(rows, jnp.dot(a_ref[rows, :], w, preferred_element_type=F32))

    def first(rows, p):
        acc_ref[rows, :] = p

    def middle(rows, p):
        acc_ref[rows, :] += p

    def final(rows, p):
        o_ref[rows, :] = (acc_ref[rows, :] + p).astype(o_ref.dtype)

    pl.when(k == 0)(functools.partial(chunks, first))
    pl.when(jnp.logical_and(k > 0, k < last))(functools.partial(chunks, middle))
    pl.when(k == last)(functools.partial(chunks, final))


def _matmul_ktiled(a, w, *, out_dtype, tm, tn, tk, name):
    m, k = a.shape
    n = w.shape[1]
    assert k // tk >= 2
    return pl.pallas_call(
        _mmk_kernel,
        grid=(m // tm, n // tn, k // tk),
        in_specs=[pl.BlockSpec((tm, tk), lambda i, j, kk: (i, kk)),
                  pl.BlockSpec((tk, tn), lambda i, j, kk: (kk, j))],
        out_specs=pl.BlockSpec((tm, tn), lambda i, j, kk: (i, j)),
        out_shape=jax.ShapeDtypeStruct((m, n), out_dtype),
        scratch_shapes=[pltpu.VMEM((tm, tn), F32)],
        compiler_params=_params(("arbitrary", "arbitrary", "arbitrary")),
        name=name,
    )(a, w)


def _merge_kernel(ga_ref, ob_ref, wa_ref, wb_ref, sa_ref, sb_ref, o_ref):
    ya = jnp.dot(ga_ref[...], wa_ref[...].astype(BF16), preferred_element_type=F32)
    yb = jnp.dot(ob_ref[...], wb_ref[...].astype(BF16), preferred_element_type=F32)
    mix = sa_ref[...].astype(F32) * ya + sb_ref[...].astype(F32) * yb
    o_ref[...] = mix.astype(o_ref.dtype)


def _merge(ga, ob, wa, wb, proj, gate_col, tm=1024, tn=1024):
    m, k = ga.shape
    n = wa.shape[1]
    nb = n // tn
    ja = gate_col // tn
    act = pl.BlockSpec((tm, k), lambda i, j: (i, 0))
    wsp = pl.BlockSpec((k, tn), lambda i, j: (0, j))
    return pl.pallas_call(
        _merge_kernel,
        grid=(m // tm, nb),
        in_specs=[act, act, wsp, wsp,
                  pl.BlockSpec((tm, tn), lambda i, j: (i, j + ja)),
                  pl.BlockSpec((tm, tn), lambda i, j: (i, j + ja + nb))],
        out_specs=pl.BlockSpec((tm, tn), lambda i, j: (i, j)),
        out_shape=jax.ShapeDtypeStruct((m, n), BF16),
        compiler_params=_params(("arbitrary", "arbitrary")),
        name="merge",
    )(ga, ob, wa, wb, proj, proj)


def _gmlp_kernel(u_ref, v_ref, gv_ref, ws_ref, bs_ref, o_ref):
    rows = u_ref.shape[0]
    v = v_ref[...].astype(F32)
    mu = jnp.mean(v, axis=-1, keepdims=True)
    vc = v - mu
    inv = lax.rsqrt(jnp.mean(vc * vc, axis=-1, keepdims=True) + EPS)
    vn = (vc * inv * gv_ref[...]).astype(BF16)
    t = lax.broadcasted_iota(jnp.int32, (GMLP_BLOCK, GMLP_BLOCK), 0)
    s = lax.broadcasted_iota(jnp.int32, (GMLP_BLOCK, GMLP_BLOCK), 1)
    chunk_causal = (t // CHUNK) >= (s // CHUNK)
    gd = GMLP_WIDTH // GMLP_GROUPS
    for g in range(GMLP_GROUPS):
        w = jnp.where(chunk_causal, ws_ref[g], 0.0).astype(BF16)
        bias = bs_ref[g]
        for b in range(rows // GMLP_BLOCK):
            r = slice(b * GMLP_BLOCK, (b + 1) * GMLP_BLOCK)
            c = slice(g * gd, (g + 1) * gd)
            y = jnp.dot(w, vn[r, c], preferred_element_type=F32) + bias
            o_ref[r, c] = (u_ref[r, c].astype(F32) * y).astype(o_ref.dtype)


def _gmlp(zuv, g_v, w_s, b_s, rows=256):
    m = zuv.shape[0]
    half = pl.BlockSpec((rows, GMLP_WIDTH), lambda i: (i, 0))
    return pl.pallas_call(
        _gmlp_kernel,
        grid=(m // rows,),
        in_specs=[half,
                  pl.BlockSpec((rows, GMLP_WIDTH), lambda i: (i, 1)),
                  pl.BlockSpec((1, GMLP_WIDTH), lambda i: (0, 0)),
                  pl.BlockSpec((GMLP_GROUPS, GMLP_BLOCK, GMLP_BLOCK), lambda i: (0, 0, 0)),
                  pl.BlockSpec((GMLP_GROUPS, GMLP_BLOCK, 1), lambda i: (0, 0, 0))],
        out_specs=half,
        out_shape=jax.ShapeDtypeStruct((m, GMLP_WIDTH), BF16),
        compiler_params=_params(("arbitrary",)),
        name="gmlp",
    )(zuv, zuv, g_v, w_s, b_s[:, :, None])


def _sb_block(q, kj, vj, carry, upper2, mask):
    z = lax.dot_general(q, kj, (((1,), (1,)), ((), ())), preferred_element_type=F32)
    e = jnp.exp2(-jnp.abs(z))
    log_beta = jnp.minimum(z, 0.0) - jnp.log(1.0 + e) * LOG2_E
    lom = log_beta - z
    if mask is not None:
        lom = jnp.where(mask, lom, 0.0)
    hi = lom.astype(BF16)
    lo = (lom - hi.astype(F32)).astype(BF16)
    between = jnp.dot(jnp.concatenate([hi, lo], axis=1), upper2, preferred_element_type=F32)
    a = jnp.exp2(log_beta + between + carry)
    if mask is not None:
        a = jnp.where(mask, a, 0.0)
    pv = jnp.dot(a.astype(BF16), vj, preferred_element_type=F32)
    return pv, jnp.sum(lom, axis=1, keepdims=True)


def _attn_kernel(q_ref, k_ref, v_ref, c_ref, wada_ref, bada_ref, o_ref, mod_ref, acc_ref, carry_ref):
    i = pl.program_id(1)
    ada_tail = functools.partial(_ada_kernel, c_ref, wada_ref, bada_ref, mod_ref)
    row = lax.broadcasted_iota(jnp.int32, (ATT_TQ, ATT_TK), 0)
    col = lax.broadcasted_iota(jnp.int32, (ATT_TQ, ATT_TK), 1)
    upper = (row > col).astype(BF16)
    upper2 = jnp.concatenate([upper, upper], axis=0)
    causal = col < row
    heads = [slice(a * SB_HEAD_DIM, (a + 1) * SB_HEAD_DIM) for a in range(ATT_HEADS_PER_STEP)]

    def tile(a, j, carry, mask):
        off = pl.multiple_of(j * ATT_TK, ATT_TK)
        return _sb_block(q_ref[:, heads[a]], k_ref[pl.ds(off, ATT_TK), heads[a]],
                         v_ref[pl.ds(off, ATT_TK), heads[a]], carry, upper2, mask)

    no_carry = jnp.zeros((ATT_TQ, 1), F32)

    @pl.when(i == 0)
    def _():
        ada_tail()
        for a in range(ATT_HEADS_PER_STEP):
            pv, _ = tile(a, i, no_carry, causal)
            o_ref[:, heads[a]] = pv.astype(o_ref.dtype)

    @pl.when(i > 0)
    def _():
        ada_tail()
        for a in range(ATT_HEADS_PER_STEP):
            pv0, tot0 = tile(a, i, no_carry, causal)
            pv1, tot1 = tile(a, i - 1, tot0, None)
            acc_ref[a] = pv0 + pv1
            carry_ref[a] = tot0 + tot1

        def more(state):
            j, max_carry = state
            return jnp.logical_and(j >= 0, max_carry > -EXP2_UNDERFLOW)

        def sweep(state):
            j, _ = state
            for a in range(ATT_HEADS_PER_STEP):
                pv, tot = tile(a, j, carry_ref[a], None)
                acc_ref[a] += pv
                carry_ref[a] += tot
            return j - 1, jnp.max(carry_ref[...])

        lax.while_loop(more, sweep, (i - 2, jnp.max(carry_ref[...])))
        for a in range(ATT_HEADS_PER_STEP):
            o_ref[:, heads[a]] = acc_ref[a].astype(o_ref.dtype)


def _attention(qkv, q_col, c, w_ada, b_ada, ada_col):
    s = qkv.shape[0]
    assert ATT_TQ == ATT_TK and s % ATT_TQ == 0
    width = ATT_HEADS_PER_STEP * SB_HEAD_DIM
    groups = SB_HEADS // ATT_HEADS_PER_STEP
    g0 = q_col // width
    nq = s // ATT_TQ
    d, n_ada = w_ada.shape
    ada_tn = (n_ada - ada_col) // (groups * nq)
    assert ada_tn % LANES == 0 and ada_col % ada_tn == 0 and ada_tn * groups * nq == n_ada - ada_col
    a0 = ada_col // ada_tn
    return pl.pallas_call(
        _attn_kernel,
        grid=(groups, nq),
        in_specs=[pl.BlockSpec((ATT_TQ, width), lambda h, i: (i, g0 + h)),
                  pl.BlockSpec((s, width), lambda h, i: (0, g0 + groups + h)),
                  pl.BlockSpec((s, width), lambda h, i: (0, g0 + 2 * groups + h)),
                  pl.BlockSpec((1, d), lambda h, i: (0, 0)),
                  pl.BlockSpec((d, ada_tn), lambda h, i: (0, a0 + h * nq + i)),
                  pl.BlockSpec((1, ada_tn), lambda h, i: (0, a0 + h * nq + i))],
        out_specs=[pl.BlockSpec((ATT_TQ, width), lambda h, i: (i, h)),
                   pl.BlockSpec((1, ada_tn), lambda h, i: (0, h * nq + i))],
        out_shape=[jax.ShapeDtypeStruct((s, SB_WIDTH), BF16),
                   jax.ShapeDtypeStruct((1, n_ada - ada_col), F32)],
        scratch_shapes=[pltpu.VMEM((ATT_HEADS_PER_STEP, ATT_TQ, SB_HEAD_DIM), F32),
                        pltpu.VMEM((ATT_HEADS_PER_STEP, ATT_TQ, 1), F32)],
        compiler_params=_params(("arbitrary", "arbitrary")),
        name="sb_attention",
    )(qkv, qkv, qkv, c, w_ada, b_ada)


def kernel(x, c, w_ada, b_ada, g_pre_mix, w_in, g_v, w_s, b_s, w_proj_a, w_proj_b, w_o,
           g_post_mix, g_pre_mlp, w_ff1, w_ff2, g_post_mlp):
    batch, seq, d = x.shape
    assert batch == 1 and d == D_MODEL
    depth = w_ada.shape[0]
    xs = x.reshape(seq, d)
    q_scale = LOG2_E / math.sqrt(SB_HEAD_DIM)
    tn_in = 512
    q_col = 2 * GMLP_WIDTH
    gate_col = q_col + 3 * SB_WIDTH
    t_q, t_k, t_gate, t_end = [c // tn_in for c in (q_col, q_col + SB_WIDTH, gate_col, gate_col + 2 * d)]
    in_epilogues = ((0, t_q, _gelu),
                    (t_q, t_k, lambda p: p * q_scale),
                    (t_k, t_gate, lambda p: p),
                    (t_gate, t_end, _sigmoid))

    for l in range(depth):
        row = lambda v: v[None, :]
        mod_head = _adaln(c, w_ada[l], row(b_ada[l]), 2 * d)
        sh1, sc1 = mod_head[:, :d], mod_head[:, d:]

        h = _prenorm(xs, row(g_pre_mix[l]), sc1, sh1)
        proj, wa, wb, wo, wf1 = _matmul(
            h, w_in[l], epilogues=in_epilogues, out_dtype=BF16, tm=1024, tn=tn_in, name="in_proj",
            casts=(w_proj_a[l], w_proj_b[l], w_o[l], w_ff1[l]))
        ga = _gmlp(proj, row(g_v[l]), w_s[l], b_s[l])
        ob, mod_tail = _attention(proj, q_col, c, w_ada[l], row(b_ada[l]), 2 * d)
        gt1, sh2, sc2, gt2 = [mod_tail[:, n * d:(n + 1) * d] for n in range(N_MOD - 2)]
        mixin = _merge(ga, ob, wa, wb, proj, gate_col)
        mix = _matmul(mixin, wo, epilogues=((0, d // 1024, lambda p: p),),
                      out_dtype=BF16, tm=1024, tn=1024, name="w_o")
        xs, h2 = _post_pre(xs, mix, row(g_post_mix[l]), gt1, row(g_pre_mlp[l]), sc2, sh2)

        hid, wf2 = _matmul(h2, wf1, epilogues=((0, D_FF // 1024, lambda p: jnp.square(jnp.maximum(p, 0.0))),),
                           out_dtype=BF16, tm=1024, tn=1024, name="ff1", casts=(w_ff2[l],))
        ff = _matmul_ktiled(hid, wf2, out_dtype=BF16, tm=2048, tn=1024, tk=2048, name="ff2")
        xs = _post(xs, ff, row(g_post_mlp[l]), gt2)
    return xs.reshape(batch, seq, d)
```

```python
import functools
import math

import jax
import jax.numpy as jnp
from jax import lax
from jax.experimental import pallas as pl
from jax.experimental.pallas import tpu as pltpu

F32 = jnp.float32
BF16 = jnp.bfloat16

D_MODEL = 4096
CHUNK = 64
GMLP_BLOCK = 128
GMLP_GROUPS = 16
GMLP_WIDTH = D_MODEL // 2
SB_HEAD_DIM = 128
SB_HEADS = (D_MODEL // 2) // SB_HEAD_DIM
SB_WIDTH = SB_HEADS * SB_HEAD_DIM
D_FF = 4 * D_MODEL
N_MOD = 6
EPS = 1e-6
LANES = 128
MIB = 1024 * 1024
VMEM_LIMIT = 56 * MIB

ATT_TQ = 256
ATT_TK = 256
ATT_HEADS_PER_STEP = 4
LOG2_E = 1.0 / math.log(2.0)
EXP2_UNDERFLOW = 160.0


def _params(semantics, vmem=VMEM_LIMIT):
    return pltpu.CompilerParams(dimension_semantics=semantics, vmem_limit_bytes=vmem)


def _ada_kernel(c_ref, w_ref, b_ref, o_ref):
    c = c_ref[...]
    s = c * jax.nn.sigmoid(c)
    s8 = jnp.broadcast_to(s, (8, s.shape[1])).astype(BF16)
    p = jnp.dot(s8, w_ref[...].astype(BF16), preferred_element_type=F32)
    o_ref[...] = p[0:1, :] + b_ref[...]


def _adaln(c, w, b, n_out, tn=1024):
    d = w.shape[0]
    return pl.pallas_call(
        _ada_kernel,
        grid=(n_out // tn,),
        in_specs=[pl.BlockSpec((1, d), lambda j: (0, 0)),
                  pl.BlockSpec((d, tn), lambda j: (0, j)),
                  pl.BlockSpec((1, tn), lambda j: (0, j))],
        out_specs=pl.BlockSpec((1, tn), lambda j: (0, j)),
        out_shape=jax.ShapeDtypeStruct((1, n_out), F32),
        compiler_params=_params(("arbitrary",)),
        name="adaln",
    )(c, w, b)


NORM_ROWS = 16


def _by_row_chunks(n_rows, body, unroll=2):
    def step(r, carry):
        body(pl.ds(pl.multiple_of(r * NORM_ROWS, NORM_ROWS), NORM_ROWS))
        return carry

    lax.fori_loop(0, n_rows // NORM_ROWS, step, 0, unroll=unroll)


def _prenorm_kernel(x_ref, g_ref, sc_ref, sh_ref, o_ref):
    gain = g_ref[...] * (1.0 + sc_ref[...])
    shift = sh_ref[...]

    def body(rows):
        x = x_ref[rows, :]
        inv = lax.rsqrt(jnp.mean(x * x, axis=-1, keepdims=True) + EPS)
        o_ref[rows, :] = (x * inv * gain + shift).astype(o_ref.dtype)

    _by_row_chunks(x_ref.shape[0], body, unroll=4)


def _prenorm(x, g, sc, sh, tm=512):
    m, d = x.shape
    row = pl.BlockSpec((tm, d), lambda i: (i, 0))
    vec = pl.BlockSpec((1, d), lambda i: (0, 0))
    return pl.pallas_call(
        _prenorm_kernel,
        grid=(m // tm,),
        in_specs=[row, vec, vec, vec],
        out_specs=row,
        out_shape=jax.ShapeDtypeStruct((m, d), BF16),
        compiler_params=_params(("arbitrary",)),
        name="prenorm",
    )(x, g, sc, sh)


def _inv_rms(x):
    return lax.rsqrt(jnp.mean(x * x, axis=-1, keepdims=True) + EPS)


def _post_pre_kernel(x_ref, y_ref, gpost_ref, gate_ref, gpre_ref, sc_ref, sh_ref, x1_ref, h_ref):
    branch_gain = gate_ref[...] * gpost_ref[...]
    gain = gpre_ref[...] * (1.0 + sc_ref[...])
    shift = sh_ref[...]

    def body(rows):
        y = y_ref[rows, :].astype(F32)
        x1 = x_ref[rows, :] + y * _inv_rms(y) * branch_gain
        x1_ref[rows, :] = x1
        h_ref[rows, :] = (x1 * _inv_rms(x1) * gain + shift).astype(h_ref.dtype)

    _by_row_chunks(x_ref.shape[0], body, unroll=4)


def _post_pre(x, y, gpost, gate, gpre, sc, sh, tm=256):
    m, d = x.shape
    row = pl.BlockSpec((tm, d), lambda i: (i, 0))
    vec = pl.BlockSpec((1, d), lambda i: (0, 0))
    return pl.pallas_call(
        _post_pre_kernel,
        grid=(m // tm,),
        in_specs=[row, row, vec, vec, vec, vec, vec],
        out_specs=[row, row],
        out_shape=[jax.ShapeDtypeStruct((m, d), F32), jax.ShapeDtypeStruct((m, d), BF16)],
        compiler_params=_params(("arbitrary",)),
        name="post_pre",
    )(x, y, gpost, gate, gpre, sc, sh)


def _post_kernel(x_ref, y_ref, gpost_ref, gate_ref, o_ref):
    branch_gain = gate_ref[...] * gpost_ref[...]

    def body(rows):
        y = y_ref[rows, :].astype(F32)
        o_ref[rows, :] = x_ref[rows, :] + y * _inv_rms(y) * branch_gain

    _by_row_chunks(x_ref.shape[0], body, unroll=4)


def _post(x, y, gpost, gate, tm=256):
    m, d = x.shape
    row = pl.BlockSpec((tm, d), lambda i: (i, 0))
    vec = pl.BlockSpec((1, d), lambda i: (0, 0))
    return pl.pallas_call(
        _post_kernel,
        grid=(m // tm,),
        in_specs=[row, row, vec, vec],
        out_specs=row,
        out_shape=jax.ShapeDtypeStruct((m, d), F32),
        compiler_params=_params(("arbitrary",)),
        name="post",
    )(x, y, gpost, gate)


def _gelu(p):
    return 0.5 * p * (1.0 + lax.erf(p * (1.0 / math.sqrt(2.0))))


def _sigmoid(p):
    return 0.5 * jnp.tanh(0.5 * p) + 0.5


def _mm_kernel(a_ref, w_ref, *refs, epilogues, n_casts):
    cast_src = refs[:n_casts]
    o_ref = refs[n_casts]
    cast_dst = refs[n_casts + 1:]
    j = pl.program_id(1)

    def run(fn):
        p = jnp.dot(a_ref[...], w_ref[...].astype(BF16), preferred_element_type=F32)
        o_ref[...] = fn(p).astype(o_ref.dtype)
        for src, dst in zip(cast_src, cast_dst):
            dst[...] = src[...].astype(dst.dtype)

    if len(epilogues) == 1:
        run(epilogues[0][2])
        return
    for lo, hi, fn in epilogues:
        pl.when(jnp.logical_and(j >= lo, j < hi))(functools.partial(run, fn))


BF16_ROWS = 16


def _matmul(a, w, *, epilogues, out_dtype, tm, tn, name, casts=()):
    m, k = a.shape
    n = w.shape[1]
    ni, nj = m // tm, n // tn
    assert epilogues[0][0] == 0 and epilogues[-1][1] == nj
    cast_specs, cast_shapes = [], []
    for cw in casts:
        r, c = cw.shape
        rows = BF16_ROWS * pl.cdiv(r, BF16_ROWS * ni * nj)
        assert r % rows == 0
        last = r // rows - 1
        cast_specs.append(pl.BlockSpec((rows, c), lambda i, j, last=last: (jnp.minimum(i * nj + j, last), 0)))
        cast_shapes.append(jax.ShapeDtypeStruct((r, c), BF16))
    res = pl.pallas_call(
        functools.partial(_mm_kernel, epilogues=epilogues, n_casts=len(casts)),
        grid=(ni, nj),
        in_specs=[pl.BlockSpec((tm, k), lambda i, j: (i, 0)),
                  pl.BlockSpec((k, tn), lambda i, j: (0, j))] + cast_specs,
        out_specs=[pl.BlockSpec((tm, tn), lambda i, j: (i, j))] + cast_specs,
        out_shape=[jax.ShapeDtypeStruct((m, n), out_dtype)] + cast_shapes,
        compiler_params=_params(("arbitrary", "arbitrary")),
        name=name,
    )(a, w, *casts)
    return res if casts else res[0]


MMK_ROWS = 256


def _mmk_kernel(a_ref, w_ref, o_ref, acc_ref):
    k = pl.program_id(2)
    last = pl.num_programs(2) - 1

    def chunks(store):
        w = w_ref[...].astype(BF16)
        for r in range(a_ref.shape[0] // MMK_ROWS):
            rows = pl.ds(r * MMK_ROWS, MMK_ROWS)
            store(rows, jnp.dot(a_ref[rows, :], w, preferred_element_type=F32))

    def first(rows, p):
        acc_ref[rows, :] = p

    def middle(rows, p):
        acc_ref[rows, :] += p

    def final(rows, p):
        o_ref[rows, :] = (acc_ref[rows, :] + p).astype(o_ref.dtype)

    pl.when(k == 0)(functools.partial(chunks, first))
    pl.when(jnp.logical_and(k > 0, k < last))(functools.partial(chunks, middle))
    pl.when(k == last)(functools.partial(chunks, final))


def _matmul_ktiled(a, w, *, out_dtype, tm, tn, tk, name):
    m, k = a.shape
    n = w.shape[1]
    assert k // tk >= 2
    return pl.pallas_call(
        _mmk_kernel,
        grid=(m // tm, n // tn, k // tk),
        in_specs=[pl.BlockSpec((tm, tk), lambda i, j, kk: (i, kk)),
                  pl.BlockSpec((tk, tn), lambda i, j, kk: (kk, j))],
        out_specs=pl.BlockSpec((tm, tn), lambda i, j, kk: (i, j)),
        out_shape=jax.ShapeDtypeStruct((m, n), out_dtype),
        scratch_shapes=[pltpu.VMEM((tm, tn), F32)],
        compiler_params=_params(("arbitrary", "arbitrary", "arbitrary")),
        name=name,
    )(a, w)


def _merge_kernel(ga_ref, ob_ref, wa_ref, wb_ref, sa_ref, sb_ref, o_ref):
    ya = jnp.dot(ga_ref[...], wa_ref[...].astype(BF16), preferred_element_type=F32)
    yb = jnp.dot(ob_ref[...], wb_ref[...].astype(BF16), preferred_element_type=F32)
    mix = sa_ref[...].astype(F32) * ya + sb_ref[...].astype(F32) * yb
    o_ref[...] = mix.astype(o_ref.dtype)


def _merge(ga, ob, wa, wb, proj, gate_col, tm=1024, tn=1024):
    m, k = ga.shape
    n = wa.shape[1]
    nb = n // tn
    ja = gate_col // tn
    act = pl.BlockSpec((tm, k), lambda i, j: (i, 0))
    wsp = pl.BlockSpec((k, tn), lambda i, j: (0, j))
    return pl.pallas_call(
        _merge_kernel,
        grid=(m // tm, nb),
        in_specs=[act, act, wsp, wsp,
                  pl.BlockSpec((tm, tn), lambda i, j: (i, j + ja)),
                  pl.BlockSpec((tm, tn), lambda i, j: (i, j + ja + nb))],
        out_specs=pl.BlockSpec((tm, tn), lambda i, j: (i, j)),
        out_shape=jax.ShapeDtypeStruct((m, n), BF16),
        compiler_params=_params(("arbitrary", "arbitrary")),
        name="merge",
    )(ga, ob, wa, wb, proj, proj)


def _gmlp_kernel(u_ref, v_ref, gv_ref, ws_ref, bs_ref, o_ref):
    rows = u_ref.shape[0]
    v = v_ref[...].astype(F32)
    mu = jnp.mean(v, axis=-1, keepdims=True)
    vc = v - mu
    inv = lax.rsqrt(jnp.mean(vc * vc, axis=-1, keepdims=True) + EPS)
    vn = (vc * inv * gv_ref[...]).astype(BF16)
    t = lax.broadcasted_iota(jnp.int32, (GMLP_BLOCK, GMLP_BLOCK), 0)
    s = lax.broadcasted_iota(jnp.int32, (GMLP_BLOCK, GMLP_BLOCK), 1)
    chunk_causal = (t // CHUNK) >= (s // CHUNK)
    gd = GMLP_WIDTH // GMLP_GROUPS
    for g in range(GMLP_GROUPS):
        w = jnp.where(chunk_causal, ws_ref[g], 0.0).astype(BF16)
        bias = bs_ref[g]
        for b in range(rows // GMLP_BLOCK):
            r = slice(b * GMLP_BLOCK, (b + 1) * GMLP_BLOCK)
            c = slice(g * gd, (g + 1) * gd)
            y = jnp.dot(w, vn[r, c], preferred_element_type=F32) + bias
            o_ref[r, c] = (u_ref[r, c].astype(F32) * y).astype(o_ref.dtype)


def _gmlp(zuv, g_v, w_s, b_s, rows=256):
    m = zuv.shape[0]
    half = pl.BlockSpec((rows, GMLP_WIDTH), lambda i: (i, 0))
    return pl.pallas_call(
        _gmlp_kernel,
        grid=(m // rows,),
        in_specs=[half,
                  pl.BlockSpec((rows, GMLP_WIDTH), lambda i: (i, 1)),
                  pl.BlockSpec((1, GMLP_WIDTH), lambda i: (0, 0)),
                  pl.BlockSpec((GMLP_GROUPS, GMLP_BLOCK, GMLP_BLOCK), lambda i: (0, 0, 0)),
                  pl.BlockSpec((GMLP_GROUPS, GMLP_BLOCK, 1), lambda i: (0, 0, 0))],
        out_specs=half,
        out_shape=jax.ShapeDtypeStruct((m, GMLP_WIDTH), BF16),
        compiler_params=_params(("arbitrary",)),
        name="gmlp",
    )(zuv, zuv, g_v, w_s, b_s[:, :, None])


def _sb_scores(q, k):
    return lax.dot_general(q, k, (((1,), (1,)), ((), ())), preferred_element_type=F32)


def _sb_logits(z):
    e = jnp.exp2(-jnp.abs(z))
    log_beta = jnp.minimum(z, 0.0) - jnp.log(1.0 + e) * LOG2_E
    return log_beta, log_beta - z


def _hi_lo(x):
    hi = x.astype(BF16)
    lo = (x - hi.astype(F32)).astype(BF16)
    return jnp.concatenate([hi, lo], axis=1)


def _sb_block(q, kj, vj, carry, upper2, mask):
    log_beta, lom = _sb_logits(_sb_scores(q, kj))
    if mask is not None:
        lom = jnp.where(mask, lom, 0.0)
    between = jnp.dot(_hi_lo(lom), upper2, preferred_element_type=F32)
    a = jnp.exp2(log_beta + between + carry)
    if mask is not None:
        a = jnp.where(mask, a, 0.0)
    pv = jnp.dot(a.astype(BF16), vj, preferred_element_type=F32)
    return pv, jnp.sum(lom, axis=1, keepdims=True)


def _attn_kernel(q_ref, k_ref, v_ref, c_ref, wada_ref, bada_ref, cast_ref,
                 o_ref, mod_ref, cast_out_ref, acc_ref, carry_ref):
    i = pl.program_id(1)

    def side_jobs():
        _ada_kernel(c_ref, wada_ref, bada_ref, mod_ref)
        cast_out_ref[...] = cast_ref[...].astype(cast_out_ref.dtype)
    row = lax.broadcasted_iota(jnp.int32, (ATT_TQ, ATT_TK), 0)
    col = lax.broadcasted_iota(jnp.int32, (ATT_TQ, ATT_TK), 1)
    upper = (row > col).astype(BF16)
    upper2 = jnp.concatenate([upper, upper], axis=0)
    causal = col < row
    heads = [slice(a * SB_HEAD_DIM, (a + 1) * SB_HEAD_DIM) for a in range(ATT_HEADS_PER_STEP)]

    def tile(a, j, carry, mask):
        off = pl.multiple_of(j * ATT_TK, ATT_TK)
        return _sb_block(q_ref[:, heads[a]], k_ref[pl.ds(off, ATT_TK), heads[a]],
                         v_ref[pl.ds(off, ATT_TK), heads[a]], carry, upper2, mask)

    no_carry = jnp.zeros((ATT_TQ, 1), F32)

    @pl.when(i == 0)
    def _():
        side_jobs()
        for a in range(ATT_HEADS_PER_STEP):
            pv, _ = tile(a, i, no_carry, causal)
            o_ref[:, heads[a]] = pv.astype(o_ref.dtype)

    @pl.when(i > 0)
    def _():
        side_jobs()
        off = pl.multiple_of((i - 1) * ATT_TK, ATT_TK)
        prev, diag = slice(0, ATT_TK), slice(ATT_TK, 2 * ATT_TK)
        scores = [_sb_scores(q_ref[:, heads[a]], k_ref[pl.ds(off, 2 * ATT_TK), heads[a]])
                  for a in range(ATT_HEADS_PER_STEP)]
        stage1 = []
        for a in range(ATT_HEADS_PER_STEP):
            log_beta, lom = _sb_logits(scores[a])
            lom_prev = lom[:, prev]
            lom_diag = jnp.where(causal, lom[:, diag], 0.0)
            tot_diag = jnp.sum(lom_diag, axis=1, keepdims=True)
            carry_ref[a] = tot_diag + jnp.sum(lom_prev, axis=1, keepdims=True)
            stage1.append((log_beta, jnp.concatenate([_hi_lo(lom_prev), _hi_lo(lom_diag)], axis=0), tot_diag))
        stage2 = [jnp.dot(hl, upper2, preferred_element_type=F32) for _, hl, _ in stage1]
        weights = []
        for a in range(ATT_HEADS_PER_STEP):
            log_beta, _, tot_diag = stage1[a]
            between = stage2[a]
            w_prev = jnp.exp2(log_beta[:, prev] + between[:ATT_TQ] + tot_diag)
            w_diag = jnp.where(causal, jnp.exp2(log_beta[:, diag] + between[ATT_TQ:]), 0.0)
            weights.append(jnp.concatenate([w_prev, w_diag], axis=1).astype(BF16))
        for a in range(ATT_HEADS_PER_STEP):
            acc_ref[a] = jnp.dot(weights[a], v_ref[pl.ds(off, 2 * ATT_TK), heads[a]],
                                 preferred_element_type=F32)

        def more(state):
            j, max_carry = state
            return jnp.logical_and(j >= 0, max_carry > -EXP2_UNDERFLOW)

        def sweep(state):
            j, _ = state
            for a in range(ATT_HEADS_PER_STEP):
                pv, tot = tile(a, j, carry_ref[a], None)
                acc_ref[a] += pv
                carry_ref[a] += tot
            return j - 1, jnp.max(carry_ref[...])

        lax.while_loop(more, sweep, (i - 2, jnp.max(carry_ref[...])))
        for a in range(ATT_HEADS_PER_STEP):
            o_ref[:, heads[a]] = acc_ref[a].astype(o_ref.dtype)


def _attention(qkv, q_col, c, w_ada, b_ada, ada_col, cast_w):
    s = qkv.shape[0]
    assert ATT_TQ == ATT_TK and s % ATT_TQ == 0
    width = ATT_HEADS_PER_STEP * SB_HEAD_DIM
    groups = SB_HEADS // ATT_HEADS_PER_STEP
    g0 = q_col // width
    nq = s // ATT_TQ
    d, n_ada = w_ada.shape
    ada_tn = (n_ada - ada_col) // (groups * nq)
    assert ada_tn % LANES == 0 and ada_col % ada_tn == 0 and ada_tn * groups * nq == n_ada - ada_col
    a0 = ada_col // ada_tn
    cast_rows = cast_w.shape[0] // (groups * nq)
    assert cast_rows % BF16_ROWS == 0 and cast_rows * groups * nq == cast_w.shape[0]
    cast_spec = pl.BlockSpec((cast_rows, cast_w.shape[1]), lambda h, i: (h * nq + i, 0))
    return pl.pallas_call(
        _attn_kernel,
        grid=(groups, nq),
        in_specs=[pl.BlockSpec((ATT_TQ, width), lambda h, i: (i, g0 + h)),
                  pl.BlockSpec((s, width), lambda h, i: (0, g0 + groups + h)),
                  pl.BlockSpec((s, width), lambda h, i: (0, g0 + 2 * groups + h)),
                  pl.BlockSpec((1, d), lambda h, i: (0, 0)),
                  pl.BlockSpec((d, ada_tn), lambda h, i: (0, a0 + h * nq + i)),
                  pl.BlockSpec((1, ada_tn), lambda h, i: (0, a0 + h * nq + i)),
                  cast_spec],
        out_specs=[pl.BlockSpec((ATT_TQ, width), lambda h, i: (i, h)),
                   pl.BlockSpec((1, ada_tn), lambda h, i: (0, h * nq + i)),
                   cast_spec],
        out_shape=[jax.ShapeDtypeStruct((s, SB_WIDTH), BF16),
                   jax.ShapeDtypeStruct((1, n_ada - ada_col), F32),
                   jax.ShapeDtypeStruct(cast_w.shape, BF16)],
        scratch_shapes=[pltpu.VMEM((ATT_HEADS_PER_STEP, ATT_TQ, SB_HEAD_DIM), F32),
                        pltpu.VMEM((ATT_HEADS_PER_STEP, ATT_TQ, 1), F32)],
        compiler_params=_params(("arbitrary", "arbitrary")),
        name="sb_attention",
    )(qkv, qkv, qkv, c, w_ada, b_ada, cast_w)


def kernel(x, c, w_ada, b_ada, g_pre_mix, w_in, g_v, w_s, b_s, w_proj_a, w_proj_b, w_o,
           g_post_mix, g_pre_mlp, w_ff1, w_ff2, g_post_mlp):
    batch, seq, d = x.shape
    assert batch == 1 and d == D_MODEL
    depth = w_ada.shape[0]
    xs = x.reshape(seq, d)
    q_scale = LOG2_E / math.sqrt(SB_HEAD_DIM)
    tm_in, tn_in = 1024, 512
    q_col = 2 * GMLP_WIDTH
    gate_col = q_col + 3 * SB_WIDTH
    t_q, t_k, t_gate, t_end = [c // tn_in for c in (q_col, q_col + SB_WIDTH, gate_col, gate_col + 2 * d)]
    in_epilogues = ((0, t_q, _gelu),
                    (t_q, t_k, lambda p: p * q_scale),
                    (t_k, t_gate, lambda p: p),
                    (t_gate, t_end, _sigmoid))

    for l in range(depth):
        row = lambda v: v[None, :]
        mod_head = _adaln(c, w_ada[l], row(b_ada[l]), 2 * d)
        sh1, sc1 = mod_head[:, :d], mod_head[:, d:]

        h = _prenorm(xs, row(g_pre_mix[l]), sc1, sh1)
        proj, wa, wb, wo = _matmul(
            h, w_in[l], epilogues=in_epilogues, out_dtype=BF16, tm=tm_in, tn=tn_in, name="in_proj",
            casts=(w_proj_a[l], w_proj_b[l], w_o[l]))
        ga = _gmlp(proj, row(g_v[l]), w_s[l], b_s[l])
        ob, mod_tail, wf1 = _attention(proj, q_col, c, w_ada[l], row(b_ada[l]), 2 * d, w_ff1[l])
        gt1, sh2, sc2, gt2 = [mod_tail[:, n * d:(n + 1) * d] for n in range(N_MOD - 2)]
        mixin = _merge(ga, ob, wa, wb, proj, gate_col)
        mix = _matmul(mixin, wo, epilogues=((0, d // 1024, lambda p: p),),
                      out_dtype=BF16, tm=1024, tn=1024, name="w_o")
        xs, h2 = _post_pre(xs, mix, row(g_post_mix[l]), gt1, row(g_pre_mlp[l]), sc2, sh2)

        hid, wf2 = _matmul(h2, wf1, epilogues=((0, D_FF // 1024, lambda p: jnp.square(jnp.maximum(p, 0.0))),),
                           out_dtype=BF16, tm=1024, tn=1024, name="ff1", casts=(w_ff2[l],))
        ff = _matmul_ktiled(hid, wf2, out_dtype=BF16, tm=2048, tn=1024, tk=2048, name="ff2")
        xs = _post(xs, ff, row(g_post_mlp[l]), gt2)
    return xs.reshape(batch, seq, d)
```

```python
import functools
import math

import jax
import jax.numpy as jnp
from jax import lax
from jax.experimental import pallas as pl
from jax.experimental.pallas import tpu as pltpu

F32 = jnp.float32
BF16 = jnp.bfloat16

D_MODEL = 4096
CHUNK = 64
GMLP_BLOCK = 128
GMLP_GROUPS = 16
GMLP_WIDTH = D_MODEL // 2
SB_HEAD_DIM = 128
SB_HEADS = (D_MODEL // 2) // SB_HEAD_DIM
SB_WIDTH = SB_HEADS * SB_HEAD_DIM
D_FF = 4 * D_MODEL
N_MOD = 6
EPS = 1e-6
LANES = 128
MIB = 1024 * 1024
VMEM_LIMIT = 56 * MIB

ATT_TQ = 256
ATT_TK = 256
ATT_HEADS_PER_STEP = 4
LOG2_E = 1.0 / math.log(2.0)
EXP2_UNDERFLOW = 160.0


def _params(semantics, vmem=VMEM_LIMIT):
    return pltpu.CompilerParams(dimension_semantics=semantics, vmem_limit_bytes=vmem)


def _ada_kernel(c_ref, w_ref, b_ref, o_ref):
    c = c_ref[...]
    s = c * jax.nn.sigmoid(c)
    s8 = jnp.broadcast_to(s, (8, s.shape[1])).astype(BF16)
    p = jnp.dot(s8, w_ref[...].astype(BF16), preferred_element_type=F32)
    o_ref[...] = p[0:1, :] + b_ref[...]


def _adaln(c, w, b, n_out, tn=512):
    d = w.shape[0]
    return pl.pallas_call(
        _ada_kernel,
        grid=(n_out // tn,),
        in_specs=[pl.BlockSpec((1, d), lambda j: (0, 0)),
                  pl.BlockSpec((d, tn), lambda j: (0, j)),
                  pl.BlockSpec((1, tn), lambda j: (0, j))],
        out_specs=pl.BlockSpec((1, tn), lambda j: (0, j)),
        out_shape=jax.ShapeDtypeStruct((1, n_out), F32),
        compiler_params=_params(("arbitrary",)),
        name="adaln",
    )(c, w, b)


def _rms(x, g):
    inv = lax.rsqrt(jnp.mean(x * x, axis=-1, keepdims=True) + EPS)
    return x * inv * g


def _prenorm_kernel(x_ref, g_ref, sc_ref, sh_ref, o_ref):
    h = _rms(x_ref[...], g_ref[...]) * (1.0 + sc_ref[...]) + sh_ref[...]
    o_ref[...] = h.astype(o_ref.dtype)


def _prenorm(x, g, sc, sh, tm=256):
    m, d = x.shape
    row = pl.BlockSpec((tm, d), lambda i: (i, 0))
    vec = pl.BlockSpec((1, d), lambda i: (0, 0))
    return pl.pallas_call(
        _prenorm_kernel,
        grid=(m // tm,),
        in_specs=[row, vec, vec, vec],
        out_specs=row,
        out_shape=jax.ShapeDtypeStruct((m, d), BF16),
        compiler_params=_params(("arbitrary",)),
        name="prenorm",
    )(x, g, sc, sh)


def _post_pre_kernel(x_ref, y_ref, gpost_ref, gate_ref, gpre_ref, sc_ref, sh_ref, x1_ref, h_ref):
    x1 = x_ref[...] + gate_ref[...] * _rms(y_ref[...].astype(F32), gpost_ref[...])
    x1_ref[...] = x1
    h = _rms(x1, gpre_ref[...]) * (1.0 + sc_ref[...]) + sh_ref[...]
    h_ref[...] = h.astype(h_ref.dtype)


def _post_pre(x, y, gpost, gate, gpre, sc, sh, tm=256):
    m, d = x.shape
    row = pl.BlockSpec((tm, d), lambda i: (i, 0))
    vec = pl.BlockSpec((1, d), lambda i: (0, 0))
    return pl.pallas_call(
        _post_pre_kernel,
        grid=(m // tm,),
        in_specs=[row, row, vec, vec, vec, vec, vec],
        out_specs=[row, row],
        out_shape=[jax.ShapeDtypeStruct((m, d), F32), jax.ShapeDtypeStruct((m, d), BF16)],
        compiler_params=_params(("arbitrary",)),
        name="post_pre",
    )(x, y, gpost, gate, gpre, sc, sh)


def _post_kernel(x_ref, y_ref, gpost_ref, gate_ref, o_ref):
    o_ref[...] = x_ref[...] + gate_ref[...] * _rms(y_ref[...].astype(F32), gpost_ref[...])


def _post(x, y, gpost, gate, tm=256):
    m, d = x.shape
    row = pl.BlockSpec((tm, d), lambda i: (i, 0))
    vec = pl.BlockSpec((1, d), lambda i: (0, 0))
    return pl.pallas_call(
        _post_kernel,
        grid=(m // tm,),
        in_specs=[row, row, vec, vec],
        out_specs=row,
        out_shape=jax.ShapeDtypeStruct((m, d), F32),
        compiler_params=_params(("arbitrary",)),
        name="post",
    )(x, y, gpost, gate)


def _gelu(p):
    return 0.5 * p * (1.0 + lax.erf(p * (1.0 / math.sqrt(2.0))))


def _sigmoid(p):
    return 0.5 * jnp.tanh(0.5 * p) + 0.5


def _mm_kernel(a_ref, w_ref, *refs, epilogues, n_casts):
    cast_src = refs[:n_casts]
    o_ref = refs[n_casts]
    cast_dst = refs[n_casts + 1:]
    j = pl.program_id(1)

    def run(fn):
        p = jnp.dot(a_ref[...], w_ref[...].astype(BF16), preferred_element_type=F32)
        o_ref[...] = fn(p).astype(o_ref.dtype)
        for src, dst in zip(cast_src, cast_dst):
            dst[...] = src[...].astype(dst.dtype)

    if len(epilogues) == 1:
        run(epilogues[0][2])
        return
    for lo, hi, fn in epilogues:
        pl.when(jnp.logical_and(j >= lo, j < hi))(functools.partial(run, fn))


BF16_ROWS = 16


def _matmul(a, w, *, epilogues, out_dtype, tm, tn, name, casts=()):
    m, k = a.shape
    n = w.shape[1]
    ni, nj = m // tm, n // tn
    assert epilogues[0][0] == 0 and epilogues[-1][1] == nj
    cast_specs, cast_shapes = [], []
    for cw in casts:
        r, c = cw.shape
        rows = BF16_ROWS * pl.cdiv(r, BF16_ROWS * ni * nj)
        assert r % rows == 0
        last = r // rows - 1
        cast_specs.append(pl.BlockSpec((rows, c), lambda i, j, last=last: (jnp.minimum(i * nj + j, last), 0)))
        cast_shapes.append(jax.ShapeDtypeStruct((r, c), BF16))
    res = pl.pallas_call(
        functools.partial(_mm_kernel, epilogues=epilogues, n_casts=len(casts)),
        grid=(ni, nj),
        in_specs=[pl.BlockSpec((tm, k), lambda i, j: (i, 0)),
                  pl.BlockSpec((k, tn), lambda i, j: (0, j))] + cast_specs,
        out_specs=[pl.BlockSpec((tm, tn), lambda i, j: (i, j))] + cast_specs,
        out_shape=[jax.ShapeDtypeStruct((m, n), out_dtype)] + cast_shapes,
        compiler_params=_params(("arbitrary", "arbitrary")),
        name=name,
    )(a, w, *casts)
    return res if casts else res[0]


MMK_ROWS = 256


def _mmk_kernel(a_ref, w_ref, o_ref, acc_ref):
    k = pl.program_id(2)
    last = pl.num_programs(2) - 1

    def chunks(store):
        w = w_ref[...].astype(BF16)
        for r in range(a_ref.shape[0] // MMK_ROWS):
            rows = pl.ds(r * MMK_ROWS, MMK_ROWS)
            store(rows, jnp.dot(a_ref[rows, :], w, preferred_element_type=F32))

    def first(rows, p):
        acc_ref[rows, :] = p

    def middle(rows, p):
        acc_ref[rows, :] += p

    def final(rows, p):
        o_ref[rows, :] = (acc_ref[rows, :] + p).astype(o_ref.dtype)

    pl.when(k == 0)(functools.partial(chunks, first))
    pl.when(jnp.logical_and(k > 0, k < last))(functools.partial(chunks, middle))
    pl.when(k == last)(functools.partial(chunks, final))


def _matmul_ktiled(a, w, *, out_dtype, tm, tn, tk, name):
    m, k = a.shape
    n = w.shape[1]
    assert k // tk >= 2
    return pl.pallas_call(
        _mmk_kernel,
        grid=(m // tm, n // tn, k // tk),
        in_specs=[pl.BlockSpec((tm, tk), lambda i, j, kk: (i, kk)),
                  pl.BlockSpec((tk, tn), lambda i, j, kk: (kk, j))],
        out_specs=pl.BlockSpec((tm, tn), lambda i, j, kk: (i, j)),
        out_shape=jax.ShapeDtypeStruct((m, n), out_dtype),
        scratch_shapes=[pltpu.VMEM((tm, tn), F32)],
        compiler_params=_params(("arbitrary", "arbitrary", "arbitrary")),
        name=name,
    )(a, w)


def _merge_kernel(ga_ref, ob_ref, wa_ref, wb_ref, sa_ref, sb_ref, o_ref):
    ya = jnp.dot(ga_ref[...], wa_ref[...].astype(BF16), preferred_element_type=F32)
    yb = jnp.dot(ob_ref[...], wb_ref[...].astype(BF16), preferred_element_type=F32)
    mix = sa_ref[...].astype(F32) * ya + sb_ref[...].astype(F32) * yb
    o_ref[...] = mix.astype(o_ref.dtype)


def _merge(ga, ob, wa, wb, proj, gate_col, tm=1024, tn=1024):
    m, k = ga.shape
    n = wa.shape[1]
    nb = n // tn
    ja = gate_col // tn
    act = pl.BlockSpec((tm, k), lambda i, j: (i, 0))
    wsp = pl.BlockSpec((k, tn), lambda i, j: (0, j))
    return pl.pallas_call(
        _merge_kernel,
        grid=(m // tm, nb),
        in_specs=[act, act, wsp, wsp,
                  pl.BlockSpec((tm, tn), lambda i, j: (i, j + ja)),
                  pl.BlockSpec((tm, tn), lambda i, j: (i, j + ja + nb))],
        out_specs=pl.BlockSpec((tm, tn), lambda i, j: (i, j)),
        out_shape=jax.ShapeDtypeStruct((m, n), BF16),
        compiler_params=_params(("arbitrary", "arbitrary")),
        name="merge",
    )(ga, ob, wa, wb, proj, proj)


def _gmlp_kernel(u_ref, v_ref, gv_ref, ws_ref, bs_ref, o_ref):
    rows = u_ref.shape[0]
    v = v_ref[...].astype(F32)
    mu = jnp.mean(v, axis=-1, keepdims=True)
    vc = v - mu
    inv = lax.rsqrt(jnp.mean(vc * vc, axis=-1, keepdims=True) + EPS)
    vn = (vc * inv * gv_ref[...]).astype(BF16)
    t = lax.broadcasted_iota(jnp.int32, (GMLP_BLOCK, GMLP_BLOCK), 0)
    s = lax.broadcasted_iota(jnp.int32, (GMLP_BLOCK, GMLP_BLOCK), 1)
    chunk_causal = (t // CHUNK) >= (s // CHUNK)
    gd = GMLP_WIDTH // GMLP_GROUPS
    for g in range(GMLP_GROUPS):
        w = jnp.where(chunk_causal, ws_ref[g], 0.0).astype(BF16)
        bias = bs_ref[g]
        for b in range(rows // GMLP_BLOCK):
            r = slice(b * GMLP_BLOCK, (b + 1) * GMLP_BLOCK)
            c = slice(g * gd, (g + 1) * gd)
            y = jnp.dot(w, vn[r, c], preferred_element_type=F32) + bias
            o_ref[r, c] = (u_ref[r, c].astype(F32) * y).astype(o_ref.dtype)


def _gmlp(zuv, g_v, w_s, b_s, rows=256):
    m = zuv.shape[0]
    half = pl.BlockSpec((rows, GMLP_WIDTH), lambda i: (i, 0))
    return pl.pallas_call(
        _gmlp_kernel,
        grid=(m // rows,),
        in_specs=[half,
                  pl.BlockSpec((rows, GMLP_WIDTH), lambda i: (i, 1)),
                  pl.BlockSpec((1, GMLP_WIDTH), lambda i: (0, 0)),
                  pl.BlockSpec((GMLP_GROUPS, GMLP_BLOCK, GMLP_BLOCK), lambda i: (0, 0, 0)),
                  pl.BlockSpec((GMLP_GROUPS, GMLP_BLOCK, 1), lambda i: (0, 0, 0))],
        out_specs=half,
        out_shape=jax.ShapeDtypeStruct((m, GMLP_WIDTH), BF16),
        compiler_params=_params(("arbitrary",)),
        name="gmlp",
    )(zuv, zuv, g_v, w_s, b_s[:, :, None])


def _sb_scores(q, k):
    return lax.dot_general(q, k, (((1,), (1,)), ((), ())), preferred_element_type=F32)


def _sb_logits(z):
    e = jnp.exp2(-jnp.abs(z))
    log_beta = jnp.minimum(z, 0.0) - jnp.log(1.0 + e) * LOG2_E
    return log_beta, log_beta - z


def _hi_lo(x):
    hi = x.astype(BF16)
    lo = (x - hi.astype(F32)).astype(BF16)
    return jnp.concatenate([hi, lo], axis=1)


def _sb_block(q, kj, vj, carry, upper2, mask):
    log_beta, lom = _sb_logits(_sb_scores(q, kj))
    if mask is not None:
        lom = jnp.where(mask, lom, 0.0)
    between = jnp.dot(_hi_lo(lom), upper2, preferred_element_type=F32)
    a = jnp.exp2(log_beta + between + carry)
    if mask is not None:
        a = jnp.where(mask, a, 0.0)
    pv = jnp.dot(a.astype(BF16), vj, preferred_element_type=F32)
    return pv, jnp.sum(lom, axis=1, keepdims=True)


def _attn_kernel(q_ref, k_ref, v_ref, c_ref, wada_ref, bada_ref, o_ref, mod_ref, acc_ref, carry_ref):
    i = pl.program_id(1)
    ada_tail = functools.partial(_ada_kernel, c_ref, wada_ref, bada_ref, mod_ref)
    row = lax.broadcasted_iota(jnp.int32, (ATT_TQ, ATT_TK), 0)
    col = lax.broadcasted_iota(jnp.int32, (ATT_TQ, ATT_TK), 1)
    upper = (row > col).astype(BF16)
    upper2 = jnp.concatenate([upper, upper], axis=0)
    causal = col < row
    heads = [slice(a * SB_HEAD_DIM, (a + 1) * SB_HEAD_DIM) for a in range(ATT_HEADS_PER_STEP)]

    def tile(a, j, carry, mask):
        off = pl.multiple_of(j * ATT_TK, ATT_TK)
        return _sb_block(q_ref[:, heads[a]], k_ref[pl.ds(off, ATT_TK), heads[a]],
                         v_ref[pl.ds(off, ATT_TK), heads[a]], carry, upper2, mask)

    no_carry = jnp.zeros((ATT_TQ, 1), F32)

    @pl.when(i == 0)
    def _():
        ada_tail()
        for a in range(ATT_HEADS_PER_STEP):
            pv, _ = tile(a, i, no_carry, causal)
            o_ref[:, heads[a]] = pv.astype(o_ref.dtype)

    @pl.when(i > 0)
    def _():
        ada_tail()
        off = pl.multiple_of((i - 1) * ATT_TK, ATT_TK)
        prev, diag = slice(0, ATT_TK), slice(ATT_TK, 2 * ATT_TK)
        scores = [_sb_scores(q_ref[:, heads[a]], k_ref[pl.ds(off, 2 * ATT_TK), heads[a]])
                  for a in range(ATT_HEADS_PER_STEP)]
        stage1 = []
        for a in range(ATT_HEADS_PER_STEP):
            log_beta, lom = _sb_logits(scores[a])
            lom_prev = lom[:, prev]
            lom_diag = jnp.where(causal, lom[:, diag], 0.0)
            tot_diag = jnp.sum(lom_diag, axis=1, keepdims=True)
            carry_ref[a] = tot_diag + jnp.sum(lom_prev, axis=1, keepdims=True)
            stage1.append((log_beta, jnp.concatenate([_hi_lo(lom_prev), _hi_lo(lom_diag)], axis=0), tot_diag))
        stage2 = [jnp.dot(hl, upper2, preferred_element_type=F32) for _, hl, _ in stage1]
        weights = []
        for a in range(ATT_HEADS_PER_STEP):
            log_beta, _, tot_diag = stage1[a]
            between = stage2[a]
            w_prev = jnp.exp2(log_beta[:, prev] + between[:ATT_TQ] + tot_diag)
            w_diag = jnp.where(causal, jnp.exp2(log_beta[:, diag] + between[ATT_TQ:]), 0.0)
            weights.append(jnp.concatenate([w_prev, w_diag], axis=1).astype(BF16))
        for a in range(ATT_HEADS_PER_STEP):
            acc_ref[a] = jnp.dot(weights[a], v_ref[pl.ds(off, 2 * ATT_TK), heads[a]],
                                 preferred_element_type=F32)

        def more(state):
            j, max_carry = state
            return jnp.logical_and(j >= 0, max_carry > -EXP2_UNDERFLOW)

        def sweep(state):
            j, _ = state
            for a in range(ATT_HEADS_PER_STEP):
                pv, tot = tile(a, j, carry_ref[a], None)
                acc_ref[a] += pv
                carry_ref[a] += tot
            return j - 1, jnp.max(carry_ref[...])

        lax.while_loop(more, sweep, (i - 2, jnp.max(carry_ref[...])))
        for a in range(ATT_HEADS_PER_STEP):
            o_ref[:, heads[a]] = acc_ref[a].astype(o_ref.dtype)


def _attention(qkv, q_col, c, w_ada, b_ada, ada_col):
    s = qkv.shape[0]
    assert ATT_TQ == ATT_TK and s % ATT_TQ == 0
    width = ATT_HEADS_PER_STEP * SB_HEAD_DIM
    groups = SB_HEADS // ATT_HEADS_PER_STEP
    g0 = q_col // width
    nq = s // ATT_TQ
    d, n_ada = w_ada.shape
    ada_tn = (n_ada - ada_col) // (groups * nq)
    assert ada_tn % LANES == 0 and ada_col % ada_tn == 0 and ada_tn * groups * nq == n_ada - ada_col
    a0 = ada_col // ada_tn
    return pl.pallas_call(
        _attn_kernel,
        grid=(groups, nq),
        in_specs=[pl.BlockSpec((ATT_TQ, width), lambda h, i: (i, g0 + h)),
                  pl.BlockSpec((s, width), lambda h, i: (0, g0 + groups + h)),
                  pl.BlockSpec((s, width), lambda h, i: (0, g0 + 2 * groups + h)),
                  pl.BlockSpec((1, d), lambda h, i: (0, 0)),
                  pl.BlockSpec((d, ada_tn), lambda h, i: (0, a0 + h * nq + i)),
                  pl.BlockSpec((1, ada_tn), lambda h, i: (0, a0 + h * nq + i))],
        out_specs=[pl.BlockSpec((ATT_TQ, width), lambda h, i: (i, h)),
                   pl.BlockSpec((1, ada_tn), lambda h, i: (0, h * nq + i))],
        out_shape=[jax.ShapeDtypeStruct((s, SB_WIDTH), BF16),
                   jax.ShapeDtypeStruct((1, n_ada - ada_col), F32)],
        scratch_shapes=[pltpu.VMEM((ATT_HEADS_PER_STEP, ATT_TQ, SB_HEAD_DIM), F32),
                        pltpu.VMEM((ATT_HEADS_PER_STEP, ATT_TQ, 1), F32)],
        compiler_params=_params(("arbitrary", "arbitrary")),
        name="sb_attention",
    )(qkv, qkv, qkv, c, w_ada, b_ada)


def kernel(x, c, w_ada, b_ada, g_pre_mix, w_in, g_v, w_s, b_s, w_proj_a, w_proj_b, w_o,
           g_post_mix, g_pre_mlp, w_ff1, w_ff2, g_post_mlp):
    batch, seq, d = x.shape
    assert batch == 1 and d == D_MODEL
    depth = w_ada.shape[0]
    xs = x.reshape(seq, d)
    q_scale = LOG2_E / math.sqrt(SB_HEAD_DIM)
    tm_in, tn_in = 1024, 512
    q_col = 2 * GMLP_WIDTH
    gate_col = q_col + 3 * SB_WIDTH
    t_q, t_k, t_gate, t_end = [c // tn_in for c in (q_col, q_col + SB_WIDTH, gate_col, gate_col + 2 * d)]
    in_epilogues = ((0, t_q, _gelu),
                    (t_q, t_k, lambda p: p * q_scale),
                    (t_k, t_gate, lambda p: p),
                    (t_gate, t_end, _sigmoid))

    for l in range(depth):
        row = lambda v: v[None, :]
        mod_head = _adaln(c, w_ada[l], row(b_ada[l]), 2 * d)
        sh1, sc1 = mod_head[:, :d], mod_head[:, d:]

        h = _prenorm(xs, row(g_pre_mix[l]), sc1, sh1)
        proj, wa, wb, wo, wf1 = _matmul(
            h, w_in[l], epilogues=in_epilogues, out_dtype=BF16, tm=tm_in, tn=tn_in, name="in_proj",
            casts=(w_proj_a[l], w_proj_b[l], w_o[l], w_ff1[l]))
        ga = _gmlp(proj, row(g_v[l]), w_s[l], b_s[l])
        ob, mod_tail = _attention(proj, q_col, c, w_ada[l], row(b_ada[l]), 2 * d)
        gt1, sh2, sc2, gt2 = [mod_tail[:, n * d:(n + 1) * d] for n in range(N_MOD - 2)]
        mixin = _merge(ga, ob, wa, wb, proj, gate_col)
        mix = _matmul(mixin, wo, epilogues=((0, d // 1024, lambda p: p),),
                      out_dtype=BF16, tm=1024, tn=1024, name="w_o")
        xs, h2 = _post_pre(xs, mix, row(g_post_mix[l]), gt1, row(g_pre_mlp[l]), sc2, sh2)

        hid, wf2 = _matmul(h2, wf1, epilogues=((0, D_FF // 1024, lambda p: jnp.square(jnp.maximum(p, 0.0))),),
                           out_dtype=BF16, tm=1024, tn=1024, name="ff1", casts=(w_ff2[l],))
        ff = _matmul_ktiled(hid, wf2, out_dtype=BF16, tm=2048, tn=1024, tk=2048, name="ff2")
        xs = _post(xs, ff, row(g_post_mlp[l]), gt2)
    return xs.reshape(batch, seq, d)
```

```python
import functools
import math

import jax
import jax.numpy as jnp
from jax import lax
from jax.experimental import pallas as pl
from jax.experimental.pallas import tpu as pltpu

F32 = jnp.float32
BF16 = jnp.bfloat16

D_MODEL = 4096
CHUNK = 64
GMLP_BLOCK = 128
GMLP_GROUPS = 16
GMLP_WIDTH = D_MODEL // 2
SB_HEAD_DIM = 128
SB_HEADS = (D_MODEL // 2) // SB_HEAD_DIM
SB_WIDTH = SB_HEADS * SB_HEAD_DIM
D_FF = 4 * D_MODEL
N_MOD = 6
EPS = 1e-6
LANES = 128
MIB = 1024 * 1024
VMEM_LIMIT = 56 * MIB

ATT_TQ = 256
ATT_TK = 256
ATT_HEADS_PER_STEP = 4
LOG2_E = 1.0 / math.log(2.0)
EXP2_UNDERFLOW = 160.0


def _params(semantics, vmem=VMEM_LIMIT):
    return pltpu.CompilerParams(dimension_semantics=semantics, vmem_limit_bytes=vmem)


def _ada_kernel(c_ref, w_ref, b_ref, o_ref):
    c = c_ref[...]
    s = c * jax.nn.sigmoid(c)
    s8 = jnp.broadcast_to(s, (8, s.shape[1])).astype(BF16)
    p = jnp.dot(s8, w_ref[...].astype(BF16), preferred_element_type=F32)
    o_ref[...] = p[0:1, :] + b_ref[...]


def _adaln(c, w, b, n_out, tn=256):
    d = w.shape[0]
    return pl.pallas_call(
        _ada_kernel,
        grid=(n_out // tn,),
        in_specs=[pl.BlockSpec((1, d), lambda j: (0, 0)),
                  pl.BlockSpec((d, tn), lambda j: (0, j)),
                  pl.BlockSpec((1, tn), lambda j: (0, j))],
        out_specs=pl.BlockSpec((1, tn), lambda j: (0, j)),
        out_shape=jax.ShapeDtypeStruct((1, n_out), F32),
        compiler_params=_params(("arbitrary",)),
        name="adaln",
    )(c, w, b)


def _rms(x, g):
    inv = lax.rsqrt(jnp.mean(x * x, axis=-1, keepdims=True) + EPS)
    return x * inv * g


def _prenorm_kernel(x_ref, g_ref, sc_ref, sh_ref, o_ref):
    h = _rms(x_ref[...], g_ref[...]) * (1.0 + sc_ref[...]) + sh_ref[...]
    o_ref[...] = h.astype(o_ref.dtype)


def _prenorm(x, g, sc, sh, tm=512):
    m, d = x.shape
    row = pl.BlockSpec((tm, d), lambda i: (i, 0))
    vec = pl.BlockSpec((1, d), lambda i: (0, 0))
    return pl.pallas_call(
        _prenorm_kernel,
        grid=(m // tm,),
        in_specs=[row, vec, vec, vec],
        out_specs=row,
        out_shape=jax.ShapeDtypeStruct((m, d), BF16),
        compiler_params=_params(("arbitrary",)),
        name="prenorm",
    )(x, g, sc, sh)


def _post_pre_kernel(x_ref, y_ref, gpost_ref, gate_ref, gpre_ref, sc_ref, sh_ref, x1_ref, h_ref):
    x1 = x_ref[...] + gate_ref[...] * _rms(y_ref[...].astype(F32), gpost_ref[...])
    x1_ref[...] = x1
    h = _rms(x1, gpre_ref[...]) * (1.0 + sc_ref[...]) + sh_ref[...]
    h_ref[...] = h.astype(h_ref.dtype)


def _post_pre(x, y, gpost, gate, gpre, sc, sh, tm=256):
    m, d = x.shape
    row = pl.BlockSpec((tm, d), lambda i: (i, 0))
    vec = pl.BlockSpec((1, d), lambda i: (0, 0))
    return pl.pallas_call(
        _post_pre_kernel,
        grid=(m // tm,),
        in_specs=[row, row, vec, vec, vec, vec, vec],
        out_specs=[row, row],
        out_shape=[jax.ShapeDtypeStruct((m, d), F32), jax.ShapeDtypeStruct((m, d), BF16)],
        compiler_params=_params(("arbitrary",)),
        name="post_pre",
    )(x, y, gpost, gate, gpre, sc, sh)


def _post_kernel(x_ref, y_ref, gpost_ref, gate_ref, o_ref):
    o_ref[...] = x_ref[...] + gate_ref[...] * _rms(y_ref[...].astype(F32), gpost_ref[...])


def _post(x, y, gpost, gate, tm=512):
    m, d = x.shape
    row = pl.BlockSpec((tm, d), lambda i: (i, 0))
    vec = pl.BlockSpec((1, d), lambda i: (0, 0))
    return pl.pallas_call(
        _post_kernel,
        grid=(m // tm,),
        in_specs=[row, row, vec, vec],
        out_specs=row,
        out_shape=jax.ShapeDtypeStruct((m, d), F32),
        compiler_params=_params(("arbitrary",)),
        name="post",
    )(x, y, gpost, gate)


def _gelu(p):
    return 0.5 * p * (1.0 + lax.erf(p * (1.0 / math.sqrt(2.0))))


def _sigmoid(p):
    return 0.5 * jnp.tanh(0.5 * p) + 0.5


def _mm_kernel(a_ref, w_ref, *refs, epilogues, n_casts):
    cast_src = refs[:n_casts]
    o_ref = refs[n_casts]
    cast_dst = refs[n_casts + 1:]
    j = pl.program_id(1)

    def run(fn):
        p = jnp.dot(a_ref[...], w_ref[...].astype(BF16), preferred_element_type=F32)
        o_ref[...] = fn(p).astype(o_ref.dtype)
        for src, dst in zip(cast_src, cast_dst):
            dst[...] = src[...].astype(dst.dtype)

    if len(epilogues) == 1:
        run(epilogues[0][2])
        return
    for lo, hi, fn in epilogues:
        pl.when(jnp.logical_and(j >= lo, j < hi))(functools.partial(run, fn))


BF16_ROWS = 16


def _matmul(a, w, *, epilogues, out_dtype, tm, tn, name, casts=()):
    m, k = a.shape
    n = w.shape[1]
    ni, nj = m // tm, n // tn
    assert epilogues[0][0] == 0 and epilogues[-1][1] == nj
    cast_specs, cast_shapes = [], []
    for cw in casts:
        r, c = cw.shape
        rows = BF16_ROWS * pl.cdiv(r, BF16_ROWS * ni * nj)
        assert r % rows == 0
        last = r // rows - 1
        cast_specs.append(pl.BlockSpec((rows, c), lambda i, j, last=last: (jnp.minimum(i * nj + j, last), 0)))
        cast_shapes.append(jax.ShapeDtypeStruct((r, c), BF16))
    res = pl.pallas_call(
        functools.partial(_mm_kernel, epilogues=epilogues, n_casts=len(casts)),
        grid=(ni, nj),
        in_specs=[pl.BlockSpec((tm, k), lambda i, j: (i, 0)),
                  pl.BlockSpec((k, tn), lambda i, j: (0, j))] + cast_specs,
        out_specs=[pl.BlockSpec((tm, tn), lambda i, j: (i, j))] + cast_specs,
        out_shape=[jax.ShapeDtypeStruct((m, n), out_dtype)] + cast_shapes,
        compiler_params=_params(("arbitrary", "arbitrary")),
        name=name,
    )(a, w, *casts)
    return res if casts else res[0]


MMK_ROWS = 256


def _mmk_kernel(a_ref, w_ref, o_ref, acc_ref):
    k = pl.program_id(2)
    last = pl.num_programs(2) - 1

    def chunks(store):
        w = w_ref[...].astype(BF16)
        for r in range(a_ref.shape[0] // MMK_ROWS):
            rows = pl.ds(r * MMK_ROWS, MMK_ROWS)
            store(rows, jnp.dot(a_ref[rows, :], w, preferred_element_type=F32))

    def first(rows, p):
        acc_ref[rows, :] = p

    def middle(rows, p):
        acc_ref[rows, :] += p

    def final(rows, p):
        o_ref[rows, :] = (acc_ref[rows, :] + p).astype(o_ref.dtype)

    pl.when(k == 0)(functools.partial(chunks, first))
    pl.when(jnp.logical_and(k > 0, k < last))(functools.partial(chunks, middle))
    pl.when(k == last)(functools.partial(chunks, final))


def _matmul_ktiled(a, w, *, out_dtype, tm, tn, tk, name):
    m, k = a.shape
    n = w.shape[1]
    assert k // tk >= 2
    return pl.pallas_call(
        _mmk_kernel,
        grid=(m // tm, n // tn, k // tk),
        in_specs=[pl.BlockSpec((tm, tk), lambda i, j, kk: (i, kk)),
                  pl.BlockSpec((tk, tn), lambda i, j, kk: (kk, j))],
        out_specs=pl.BlockSpec((tm, tn), lambda i, j, kk: (i, j)),
        out_shape=jax.ShapeDtypeStruct((m, n), out_dtype),
        scratch_shapes=[pltpu.VMEM((tm, tn), F32)],
        compiler_params=_params(("arbitrary", "arbitrary", "arbitrary")),
        name=name,
    )(a, w)


def _merge_kernel(ga_ref, ob_ref, wa_ref, wb_ref, sa_ref, sb_ref, o_ref):
    ya = jnp.dot(ga_ref[...], wa_ref[...].astype(BF16), preferred_element_type=F32)
    yb = jnp.dot(ob_ref[...], wb_ref[...].astype(BF16), preferred_element_type=F32)
    mix = sa_ref[...].astype(F32) * ya + sb_ref[...].astype(F32) * yb
    o_ref[...] = mix.astype(o_ref.dtype)


def _merge(ga, ob, wa, wb, proj, gate_col, tm=1024, tn=1024):
    m, k = ga.shape
    n = wa.shape[1]
    nb = n // tn
    ja = gate_col // tn
    act = pl.BlockSpec((tm, k), lambda i, j: (i, 0))
    wsp = pl.BlockSpec((k, tn), lambda i, j: (0, j))
    return pl.pallas_call(
        _merge_kernel,
        grid=(m // tm, nb),
        in_specs=[act, act, wsp, wsp,
                  pl.BlockSpec((tm, tn), lambda i, j: (i, j + ja)),
                  pl.BlockSpec((tm, tn), lambda i, j: (i, j + ja + nb))],
        out_specs=pl.BlockSpec((tm, tn), lambda i, j: (i, j)),
        out_shape=jax.ShapeDtypeStruct((m, n), BF16),
        compiler_params=_params(("arbitrary", "arbitrary")),
        name="merge",
    )(ga, ob, wa, wb, proj, proj)


def _gmlp_kernel(u_ref, v_ref, gv_ref, ws_ref, bs_ref, o_ref):
    rows = u_ref.shape[0]
    v = v_ref[...].astype(F32)
    mu = jnp.mean(v, axis=-1, keepdims=True)
    vc = v - mu
    inv = lax.rsqrt(jnp.mean(vc * vc, axis=-1, keepdims=True) + EPS)
    vn = (vc * inv * gv_ref[...]).astype(BF16)
    t = lax.broadcasted_iota(jnp.int32, (GMLP_BLOCK, GMLP_BLOCK), 0)
    s = lax.broadcasted_iota(jnp.int32, (GMLP_BLOCK, GMLP_BLOCK), 1)
    chunk_causal = (t // CHUNK) >= (s // CHUNK)
    gd = GMLP_WIDTH // GMLP_GROUPS
    for g in range(GMLP_GROUPS):
        w = jnp.where(chunk_causal, ws_ref[g], 0.0).astype(BF16)
        bias = bs_ref[g]
        for b in range(rows // GMLP_BLOCK):
            r = slice(b * GMLP_BLOCK, (b + 1) * GMLP_BLOCK)
            c = slice(g * gd, (g + 1) * gd)
            y = jnp.dot(w, vn[r, c], preferred_element_type=F32) + bias
            o_ref[r, c] = (u_ref[r, c].astype(F32) * y).astype(o_ref.dtype)


def _gmlp(zuv, g_v, w_s, b_s, rows=512):
    m = zuv.shape[0]
    half = pl.BlockSpec((rows, GMLP_WIDTH), lambda i: (i, 0))
    return pl.pallas_call(
        _gmlp_kernel,
        grid=(m // rows,),
        in_specs=[half,
                  pl.BlockSpec((rows, GMLP_WIDTH), lambda i: (i, 1)),
                  pl.BlockSpec((1, GMLP_WIDTH), lambda i: (0, 0)),
                  pl.BlockSpec((GMLP_GROUPS, GMLP_BLOCK, GMLP_BLOCK), lambda i: (0, 0, 0)),
                  pl.BlockSpec((GMLP_GROUPS, GMLP_BLOCK, 1), lambda i: (0, 0, 0))],
        out_specs=half,
        out_shape=jax.ShapeDtypeStruct((m, GMLP_WIDTH), BF16),
        compiler_params=_params(("arbitrary",)),
        name="gmlp",
    )(zuv, zuv, g_v, w_s, b_s[:, :, None])


def _sb_scores(q, k):
    return lax.dot_general(q, k, (((1,), (1,)), ((), ())), preferred_element_type=F32)


def _sb_logits(z):
    e = jnp.exp2(-jnp.abs(z))
    log_beta = jnp.minimum(z, 0.0) - jnp.log(1.0 + e) * LOG2_E
    return log_beta, log_beta - z


def _hi_lo(x):
    hi = x.astype(BF16)
    lo = (x - hi.astype(F32)).astype(BF16)
    return jnp.concatenate([hi, lo], axis=1)


def _sb_block(q, kj, vj, carry, upper2, mask):
    log_beta, lom = _sb_logits(_sb_scores(q, kj))
    if mask is not None:
        lom = jnp.where(mask, lom, 0.0)
    between = jnp.dot(_hi_lo(lom), upper2, preferred_element_type=F32)
    a = jnp.exp2(log_beta + between + carry)
    if mask is not None:
        a = jnp.where(mask, a, 0.0)
    pv = jnp.dot(a.astype(BF16), vj, preferred_element_type=F32)
    return pv, jnp.sum(lom, axis=1, keepdims=True)


def _attn_kernel(q_ref, k_ref, v_ref, c_ref, wada_ref, bada_ref, o_ref, mod_ref, acc_ref, carry_ref):
    i = pl.program_id(1)
    ada_tail = functools.partial(_ada_kernel, c_ref, wada_ref, bada_ref, mod_ref)
    row = lax.broadcasted_iota(jnp.int32, (ATT_TQ, ATT_TK), 0)
    col = lax.broadcasted_iota(jnp.int32, (ATT_TQ, ATT_TK), 1)
    upper = (row > col).astype(BF16)
    upper2 = jnp.concatenate([upper, upper], axis=0)
    causal = col < row
    heads = [slice(a * SB_HEAD_DIM, (a + 1) * SB_HEAD_DIM) for a in range(ATT_HEADS_PER_STEP)]

    def tile(a, j, carry, mask):
        off = pl.multiple_of(j * ATT_TK, ATT_TK)
        return _sb_block(q_ref[:, heads[a]], k_ref[pl.ds(off, ATT_TK), heads[a]],
                         v_ref[pl.ds(off, ATT_TK), heads[a]], carry, upper2, mask)

    no_carry = jnp.zeros((ATT_TQ, 1), F32)

    @pl.when(i == 0)
    def _():
        ada_tail()
        for a in range(ATT_HEADS_PER_STEP):
            pv, _ = tile(a, i, no_carry, causal)
            o_ref[:, heads[a]] = pv.astype(o_ref.dtype)

    @pl.when(i > 0)
    def _():
        ada_tail()
        off = pl.multiple_of((i - 1) * ATT_TK, ATT_TK)
        prev, diag = slice(0, ATT_TK), slice(ATT_TK, 2 * ATT_TK)
        scores = [_sb_scores(q_ref[:, heads[a]], k_ref[pl.ds(off, 2 * ATT_TK), heads[a]])
                  for a in range(ATT_HEADS_PER_STEP)]
        stage1 = []
        for a in range(ATT_HEADS_PER_STEP):
            log_beta, lom = _sb_logits(scores[a])
            lom_prev = lom[:, prev]
            lom_diag = jnp.where(causal, lom[:, diag], 0.0)
            tot_diag = jnp.sum(lom_diag, axis=1, keepdims=True)
            carry_ref[a] = tot_diag + jnp.sum(lom_prev, axis=1, keepdims=True)
            stage1.append((log_beta, jnp.concatenate([_hi_lo(lom_prev), _hi_lo(lom_diag)], axis=0), tot_diag))
        stage2 = [jnp.dot(hl, upper2, preferred_element_type=F32) for _, hl, _ in stage1]
        weights = []
        for a in range(ATT_HEADS_PER_STEP):
            log_beta, _, tot_diag = stage1[a]
            between = stage2[a]
            w_prev = jnp.exp2(log_beta[:, prev] + between[:ATT_TQ] + tot_diag)
            w_diag = jnp.where(causal, jnp.exp2(log_beta[:, diag] + between[ATT_TQ:]), 0.0)
            weights.append(jnp.concatenate([w_prev, w_diag], axis=1).astype(BF16))
        for a in range(ATT_HEADS_PER_STEP):
            acc_ref[a] = jnp.dot(weights[a], v_ref[pl.ds(off, 2 * ATT_TK), heads[a]],
                                 preferred_element_type=F32)

        def more(state):
            j, max_carry = state
            return jnp.logical_and(j >= 0, max_carry > -EXP2_UNDERFLOW)

        def sweep(state):
            j, _ = state
            for a in range(ATT_HEADS_PER_STEP):
                pv, tot = tile(a, j, carry_ref[a], None)
                acc_ref[a] += pv
                carry_ref[a] += tot
            return j - 1, jnp.max(carry_ref[...])

        lax.while_loop(more, sweep, (i - 2, jnp.max(carry_ref[...])))
        for a in range(ATT_HEADS_PER_STEP):
            o_ref[:, heads[a]] = acc_ref[a].astype(o_ref.dtype)


def _attention(qkv, q_col, c, w_ada, b_ada, ada_col):
    s = qkv.shape[0]
    assert ATT_TQ == ATT_TK and s % ATT_TQ == 0
    width = ATT_HEADS_PER_STEP * SB_HEAD_DIM
    groups = SB_HEADS // ATT_HEADS_PER_STEP
    g0 = q_col // width
    nq = s // ATT_TQ
    d, n_ada = w_ada.shape
    ada_tn = (n_ada - ada_col) // (groups * nq)
    assert ada_tn % LANES == 0 and ada_col % ada_tn == 0 and ada_tn * groups * nq == n_ada - ada_col
    a0 = ada_col // ada_tn
    return pl.pallas_call(
        _attn_kernel,
        grid=(groups, nq),
        in_specs=[pl.BlockSpec((ATT_TQ, width), lambda h, i: (i, g0 + h)),
                  pl.BlockSpec((s, width), lambda h, i: (0, g0 + groups + h)),
                  pl.BlockSpec((s, width), lambda h, i: (0, g0 + 2 * groups + h)),
                  pl.BlockSpec((1, d), lambda h, i: (0, 0)),
                  pl.BlockSpec((d, ada_tn), lambda h, i: (0, a0 + h * nq + i)),
                  pl.BlockSpec((1, ada_tn), lambda h, i: (0, a0 + h * nq + i))],
        out_specs=[pl.BlockSpec((ATT_TQ, width), lambda h, i: (i, h)),
                   pl.BlockSpec((1, ada_tn), lambda h, i: (0, h * nq + i))],
        out_shape=[jax.ShapeDtypeStruct((s, SB_WIDTH), BF16),
                   jax.ShapeDtypeStruct((1, n_ada - ada_col), F32)],
        scratch_shapes=[pltpu.VMEM((ATT_HEADS_PER_STEP, ATT_TQ, SB_HEAD_DIM), F32),
                        pltpu.VMEM((ATT_HEADS_PER_STEP, ATT_TQ, 1), F32)],
        compiler_params=_params(("arbitrary", "arbitrary")),
        name="sb_attention",
    )(qkv, qkv, qkv, c, w_ada, b_ada)


def kernel(x, c, w_ada, b_ada, g_pre_mix, w_in, g_v, w_s, b_s, w_proj_a, w_proj_b, w_o,
           g_post_mix, g_pre_mlp, w_ff1, w_ff2, g_post_mlp):
    batch, seq, d = x.shape
    assert batch == 1 and d == D_MODEL
    depth = w_ada.shape[0]
    xs = x.reshape(seq, d)
    q_scale = LOG2_E / math.sqrt(SB_HEAD_DIM)
    tm_in, tn_in = 1024, 512
    q_col = 2 * GMLP_WIDTH
    gate_col = q_col + 3 * SB_WIDTH
    t_q, t_k, t_gate, t_end = [c // tn_in for c in (q_col, q_col + SB_WIDTH, gate_col, gate_col + 2 * d)]
    in_epilogues = ((0, t_q, _gelu),
                    (t_q, t_k, lambda p: p * q_scale),
                    (t_k, t_gate, lambda p: p),
                    (t_gate, t_end, _sigmoid))

    for l in range(depth):
        row = lambda v: v[None, :]
        mod_head = _adaln(c, w_ada[l], row(b_ada[l]), 2 * d)
        sh1, sc1 = mod_head[:, :d], mod_head[:, d:]

        h = _prenorm(xs, row(g_pre_mix[l]), sc1, sh1)
        proj, wa, wb, wo, wf1 = _matmul(
            h, w_in[l], epilogues=in_epilogues, out_dtype=BF16, tm=tm_in, tn=tn_in, name="in_proj",
            casts=(w_proj_a[l], w_proj_b[l], w_o[l], w_ff1[l]))
        ga = _gmlp(proj, row(g_v[l]), w_s[l], b_s[l])
        ob, mod_tail = _attention(proj, q_col, c, w_ada[l], row(b_ada[l]), 2 * d)
        gt1, sh2, sc2, gt2 = [mod_tail[:, n * d:(n + 1) * d] for n in range(N_MOD - 2)]
        mixin = _merge(ga, ob, wa, wb, proj, gate_col)
        mix = _matmul(mixin, wo, epilogues=((0, d // 1024, lambda p: p),),
                      out_dtype=BF16, tm=1024, tn=1024, name="w_o")
        xs, h2 = _post_pre(xs, mix, row(g_post_mix[l]), gt1, row(g_pre_mlp[l]), sc2, sh2)

        hid = _matmul(h2, wf1, epilogues=((0, D_FF // 1024, lambda p: jnp.square(jnp.maximum(p, 0.0))),),
                      out_dtype=BF16, tm=1024, tn=1024, name="ff1")
        ff = _matmul_ktiled(hid, w_ff2[l], out_dtype=BF16, tm=2048, tn=1024, tk=2048, name="ff2")
        xs = _post(xs, ff, row(g_post_mlp[l]), gt2)
    return xs.reshape(batch, seq, d)
```

```python
import functools
import math

import jax
import jax.numpy as jnp
from jax import lax
from jax.experimental import pallas as pl
from jax.experimental.pallas import tpu as pltpu

F32 = jnp.float32
BF16 = jnp.bfloat16

D_MODEL = 4096
CHUNK = 64
GMLP_BLOCK = 128
GMLP_GROUPS = 16
GMLP_WIDTH = D_MODEL // 2
SB_HEAD_DIM = 128
SB_HEADS = (D_MODEL // 2) // SB_HEAD_DIM
SB_WIDTH = SB_HEADS * SB_HEAD_DIM
D_FF = 4 * D_MODEL
N_MOD = 6
EPS = 1e-6
LANES = 128
MIB = 1024 * 1024
VMEM_LIMIT = 56 * MIB

ATT_TQ = 256
ATT_TK = 256
ATT_HEADS_PER_STEP = 4
LOG2_E = 1.0 / math.log(2.0)
EXP2_UNDERFLOW = 160.0


def _params(semantics, vmem=VMEM_LIMIT):
    return pltpu.CompilerParams(dimension_semantics=semantics, vmem_limit_bytes=vmem)


def _ada_kernel(c_ref, w_ref, b_ref, o_ref):
    c = c_ref[...]
    s = c * jax.nn.sigmoid(c)
    s8 = jnp.broadcast_to(s, (8, s.shape[1])).astype(BF16)
    p = jnp.dot(s8, w_ref[...].astype(BF16), preferred_element_type=F32)
    o_ref[...] = p[0:1, :] + b_ref[...]


def _adaln(c, w, b, n_out, tn=512):
    d = w.shape[0]
    return pl.pallas_call(
        _ada_kernel,
        grid=(n_out // tn,),
        in_specs=[pl.BlockSpec((1, d), lambda j: (0, 0)),
                  pl.BlockSpec((d, tn), lambda j: (0, j)),
                  pl.BlockSpec((1, tn), lambda j: (0, j))],
        out_specs=pl.BlockSpec((1, tn), lambda j: (0, j)),
        out_shape=jax.ShapeDtypeStruct((1, n_out), F32),
        compiler_params=_params(("arbitrary",)),
        name="adaln",
    )(c, w, b)


def _rms(x, g):
    inv = lax.rsqrt(jnp.mean(x * x, axis=-1, keepdims=True) + EPS)
    return x * inv * g


def _prenorm_kernel(x_ref, g_ref, sc_ref, sh_ref, o_ref):
    h = _rms(x_ref[...], g_ref[...]) * (1.0 + sc_ref[...]) + sh_ref[...]
    o_ref[...] = h.astype(o_ref.dtype)


def _prenorm(x, g, sc, sh, tm=512):
    m, d = x.shape
    row = pl.BlockSpec((tm, d), lambda i: (i, 0))
    vec = pl.BlockSpec((1, d), lambda i: (0, 0))
    return pl.pallas_call(
        _prenorm_kernel,
        grid=(m // tm,),
        in_specs=[row, vec, vec, vec],
        out_specs=row,
        out_shape=jax.ShapeDtypeStruct((m, d), BF16),
        compiler_params=_params(("arbitrary",)),
        name="prenorm",
    )(x, g, sc, sh)


def _post_pre_kernel(x_ref, y_ref, gpost_ref, gate_ref, gpre_ref, sc_ref, sh_ref, x1_ref, h_ref):
    x1 = x_ref[...] + gate_ref[...] * _rms(y_ref[...].astype(F32), gpost_ref[...])
    x1_ref[...] = x1
    h = _rms(x1, gpre_ref[...]) * (1.0 + sc_ref[...]) + sh_ref[...]
    h_ref[...] = h.astype(h_ref.dtype)


def _post_pre(x, y, gpost, gate, gpre, sc, sh, tm=256):
    m, d = x.shape
    row = pl.BlockSpec((tm, d), lambda i: (i, 0))
    vec = pl.BlockSpec((1, d), lambda i: (0, 0))
    return pl.pallas_call(
        _post_pre_kernel,
        grid=(m // tm,),
        in_specs=[row, row, vec, vec, vec, vec, vec],
        out_specs=[row, row],
        out_shape=[jax.ShapeDtypeStruct((m, d), F32), jax.ShapeDtypeStruct((m, d), BF16)],
        compiler_params=_params(("arbitrary",)),
        name="post_pre",
    )(x, y, gpost, gate, gpre, sc, sh)


def _post_kernel(x_ref, y_ref, gpost_ref, gate_ref, o_ref):
    o_ref[...] = x_ref[...] + gate_ref[...] * _rms(y_ref[...].astype(F32), gpost_ref[...])


def _post(x, y, gpost, gate, tm=512):
    m, d = x.shape
    row = pl.BlockSpec((tm, d), lambda i: (i, 0))
    vec = pl.BlockSpec((1, d), lambda i: (0, 0))
    return pl.pallas_call(
        _post_kernel,
        grid=(m // tm,),
        in_specs=[row, row, vec, vec],
        out_specs=row,
        out_shape=jax.ShapeDtypeStruct((m, d), F32),
        compiler_params=_params(("arbitrary",)),
        name="post",
    )(x, y, gpost, gate)


def _gelu(p):
    return 0.5 * p * (1.0 + lax.erf(p * (1.0 / math.sqrt(2.0))))


def _sigmoid(p):
    return 0.5 * jnp.tanh(0.5 * p) + 0.5


def _mm_kernel(a_ref, w_ref, *refs, epilogues, n_casts):
    cast_src = refs[:n_casts]
    o_ref = refs[n_casts]
    cast_dst = refs[n_casts + 1:]
    j = pl.program_id(1)

    def run(fn):
        p = jnp.dot(a_ref[...], w_ref[...].astype(BF16), preferred_element_type=F32)
        o_ref[...] = fn(p).astype(o_ref.dtype)
        for src, dst in zip(cast_src, cast_dst):
            dst[...] = src[...].astype(dst.dtype)

    if len(epilogues) == 1:
        run(epilogues[0][2])
        return
    for lo, hi, fn in epilogues:
        pl.when(jnp.logical_and(j >= lo, j < hi))(functools.partial(run, fn))


BF16_ROWS = 16


def _matmul(a, w, *, epilogues, out_dtype, tm, tn, name, casts=()):
    m, k = a.shape
    n = w.shape[1]
    ni, nj = m // tm, n // tn
    assert epilogues[0][0] == 0 and epilogues[-1][1] == nj
    cast_specs, cast_shapes = [], []
    for cw in casts:
        r, c = cw.shape
        rows = BF16_ROWS * pl.cdiv(r, BF16_ROWS * ni * nj)
        assert r % rows == 0
        last = r // rows - 1
        cast_specs.append(pl.BlockSpec((rows, c), lambda i, j, last=last: (jnp.minimum(i * nj + j, last), 0)))
        cast_shapes.append(jax.ShapeDtypeStruct((r, c), BF16))
    res = pl.pallas_call(
        functools.partial(_mm_kernel, epilogues=epilogues, n_casts=len(casts)),
        grid=(ni, nj),
        in_specs=[pl.BlockSpec((tm, k), lambda i, j: (i, 0)),
                  pl.BlockSpec((k, tn), lambda i, j: (0, j))] + cast_specs,
        out_specs=[pl.BlockSpec((tm, tn), lambda i, j: (i, j))] + cast_specs,
        out_shape=[jax.ShapeDtypeStruct((m, n), out_dtype)] + cast_shapes,
        compiler_params=_params(("arbitrary", "arbitrary")),
        name=name,
    )(a, w, *casts)
    return res if casts else res[0]


MMK_ROWS = 256


def _mmk_kernel(a_ref, w_ref, o_ref, acc_ref):
    k = pl.program_id(2)
    last = pl.num_programs(2) - 1

    def chunks(store):
        w = w_ref[...].astype(BF16)
        for r in range(a_ref.shape[0] // MMK_ROWS):
            rows = pl.ds(r * MMK_ROWS, MMK_ROWS)
            store(rows, jnp.dot(a_ref[rows, :], w, preferred_element_type=F32))

    def first(rows, p):
        acc_ref[rows, :] = p

    def middle(rows, p):
        acc_ref[rows, :] += p

    def final(rows, p):
        o_ref[rows, :] = (acc_ref[rows, :] + p).astype(o_ref.dtype)

    pl.when(k == 0)(functools.partial(chunks, first))
    pl.when(jnp.logical_and(k > 0, k < last))(functools.partial(chunks, middle))
    pl.when(k == last)(functools.partial(chunks, final))


def _matmul_ktiled(a, w, *, out_dtype, tm, tn, tk, name):
    m, k = a.shape
    n = w.shape[1]
    assert k // tk >= 2
    return pl.pallas_call(
        _mmk_kernel,
        grid=(m // tm, n // tn, k // tk),
        in_specs=[pl.BlockSpec((tm, tk), lambda i, j, kk: (i, kk)),
                  pl.BlockSpec((tk, tn), lambda i, j, kk: (kk, j))],
        out_specs=pl.BlockSpec((tm, tn), lambda i, j, kk: (i, j)),
        out_shape=jax.ShapeDtypeStruct((m, n), out_dtype),
        scratch_shapes=[pltpu.VMEM((tm, tn), F32)],
        compiler_params=_params(("arbitrary", "arbitrary", "arbitrary")),
        name=name,
    )(a, w)


def _merge_kernel(ga_ref, ob_ref, wa_ref, wb_ref, sa_ref, sb_ref, o_ref):
    ya = jnp.dot(ga_ref[...], wa_ref[...].astype(BF16), preferred_element_type=F32)
    yb = jnp.dot(ob_ref[...], wb_ref[...].astype(BF16), preferred_element_type=F32)
    mix = sa_ref[...].astype(F32) * ya + sb_ref[...].astype(F32) * yb
    o_ref[...] = mix.astype(o_ref.dtype)


def _merge(ga, ob, wa, wb, proj, gate_col, tm=1024, tn=1024):
    m, k = ga.shape
    n = wa.shape[1]
    nb = n // tn
    ja = gate_col // tn
    act = pl.BlockSpec((tm, k), lambda i, j: (i, 0))
    wsp = pl.BlockSpec((k, tn), lambda i, j: (0, j))
    return pl.pallas_call(
        _merge_kernel,
        grid=(m // tm, nb),
        in_specs=[act, act, wsp, wsp,
                  pl.BlockSpec((tm, tn), lambda i, j: (i, j + ja)),
                  pl.BlockSpec((tm, tn), lambda i, j: (i, j + ja + nb))],
        out_specs=pl.BlockSpec((tm, tn), lambda i, j: (i, j)),
        out_shape=jax.ShapeDtypeStruct((m, n), BF16),
        compiler_params=_params(("arbitrary", "arbitrary")),
        name="merge",
    )(ga, ob, wa, wb, proj, proj)


def _gmlp_kernel(u_ref, v_ref, gv_ref, ws_ref, bs_ref, o_ref):
    rows = u_ref.shape[0]
    v = v_ref[...].astype(F32)
    mu = jnp.mean(v, axis=-1, keepdims=True)
    vc = v - mu
    inv = lax.rsqrt(jnp.mean(vc * vc, axis=-1, keepdims=True) + EPS)
    vn = (vc * inv * gv_ref[...]).astype(BF16)
    t = lax.broadcasted_iota(jnp.int32, (GMLP_BLOCK, GMLP_BLOCK), 0)
    s = lax.broadcasted_iota(jnp.int32, (GMLP_BLOCK, GMLP_BLOCK), 1)
    chunk_causal = (t // CHUNK) >= (s // CHUNK)
    gd = GMLP_WIDTH // GMLP_GROUPS
    for g in range(GMLP_GROUPS):
        w = jnp.where(chunk_causal, ws_ref[g], 0.0).astype(BF16)
        bias = bs_ref[g]
        for b in range(rows // GMLP_BLOCK):
            r = slice(b * GMLP_BLOCK, (b + 1) * GMLP_BLOCK)
            c = slice(g * gd, (g + 1) * gd)
            y = jnp.dot(w, vn[r, c], preferred_element_type=F32) + bias
            o_ref[r, c] = (u_ref[r, c].astype(F32) * y).astype(o_ref.dtype)


def _gmlp(zuv, g_v, w_s, b_s, rows=1024):
    m = zuv.shape[0]
    half = pl.BlockSpec((rows, GMLP_WIDTH), lambda i: (i, 0))
    return pl.pallas_call(
        _gmlp_kernel,
        grid=(m // rows,),
        in_specs=[half,
                  pl.BlockSpec((rows, GMLP_WIDTH), lambda i: (i, 1)),
                  pl.BlockSpec((1, GMLP_WIDTH), lambda i: (0, 0)),
                  pl.BlockSpec((GMLP_GROUPS, GMLP_BLOCK, GMLP_BLOCK), lambda i: (0, 0, 0)),
                  pl.BlockSpec((GMLP_GROUPS, GMLP_BLOCK, 1), lambda i: (0, 0, 0))],
        out_specs=half,
        out_shape=jax.ShapeDtypeStruct((m, GMLP_WIDTH), BF16),
        compiler_params=_params(("arbitrary",)),
        name="gmlp",
    )(zuv, zuv, g_v, w_s, b_s[:, :, None])


def _sb_scores(q, k):
    return lax.dot_general(q, k, (((1,), (1,)), ((), ())), preferred_element_type=F32)


def _sb_logits(z):
    e = jnp.exp2(-jnp.abs(z))
    log_beta = jnp.minimum(z, 0.0) - jnp.log(1.0 + e) * LOG2_E
    return log_beta, log_beta - z


def _hi_lo(x):
    hi = x.astype(BF16)
    lo = (x - hi.astype(F32)).astype(BF16)
    return jnp.concatenate([hi, lo], axis=1)


def _sb_block(q, kj, vj, carry, upper2, mask):
    log_beta, lom = _sb_logits(_sb_scores(q, kj))
    if mask is not None:
        lom = jnp.where(mask, lom, 0.0)
    between = jnp.dot(_hi_lo(lom), upper2, preferred_element_type=F32)
    a = jnp.exp2(log_beta + between + carry)
    if mask is not None:
        a = jnp.where(mask, a, 0.0)
    pv = jnp.dot(a.astype(BF16), vj, preferred_element_type=F32)
    return pv, jnp.sum(lom, axis=1, keepdims=True)


def _attn_kernel(q_ref, k_ref, v_ref, c_ref, wada_ref, bada_ref, o_ref, mod_ref, acc_ref, carry_ref):
    i = pl.program_id(1)
    ada_tail = functools.partial(_ada_kernel, c_ref, wada_ref, bada_ref, mod_ref)
    row = lax.broadcasted_iota(jnp.int32, (ATT_TQ, ATT_TK), 0)
    col = lax.broadcasted_iota(jnp.int32, (ATT_TQ, ATT_TK), 1)
    upper = (row > col).astype(BF16)
    upper2 = jnp.concatenate([upper, upper], axis=0)
    causal = col < row
    heads = [slice(a * SB_HEAD_DIM, (a + 1) * SB_HEAD_DIM) for a in range(ATT_HEADS_PER_STEP)]

    def tile(a, j, carry, mask):
        off = pl.multiple_of(j * ATT_TK, ATT_TK)
        return _sb_block(q_ref[:, heads[a]], k_ref[pl.ds(off, ATT_TK), heads[a]],
                         v_ref[pl.ds(off, ATT_TK), heads[a]], carry, upper2, mask)

    no_carry = jnp.zeros((ATT_TQ, 1), F32)

    @pl.when(i == 0)
    def _():
        ada_tail()
        for a in range(ATT_HEADS_PER_STEP):
            pv, _ = tile(a, i, no_carry, causal)
            o_ref[:, heads[a]] = pv.astype(o_ref.dtype)

    @pl.when(i > 0)
    def _():
        ada_tail()
        off = pl.multiple_of((i - 1) * ATT_TK, ATT_TK)
        prev, diag = slice(0, ATT_TK), slice(ATT_TK, 2 * ATT_TK)
        scores = [_sb_scores(q_ref[:, heads[a]], k_ref[pl.ds(off, 2 * ATT_TK), heads[a]])
                  for a in range(ATT_HEADS_PER_STEP)]
        stage1 = []
        for a in range(ATT_HEADS_PER_STEP):
            log_beta, lom = _sb_logits(scores[a])
            lom_prev = lom[:, prev]
            lom_diag = jnp.where(causal, lom[:, diag], 0.0)
            tot_diag = jnp.sum(lom_diag, axis=1, keepdims=True)
            carry_ref[a] = tot_diag + jnp.sum(lom_prev, axis=1, keepdims=True)
            stage1.append((log_beta, jnp.concatenate([_hi_lo(lom_prev), _hi_lo(lom_diag)], axis=0), tot_diag))
        stage2 = [jnp.dot(hl, upper2, preferred_element_type=F32) for _, hl, _ in stage1]
        weights = []
        for a in range(ATT_HEADS_PER_STEP):
            log_beta, _, tot_diag = stage1[a]
            between = stage2[a]
            w_prev = jnp.exp2(log_beta[:, prev] + between[:ATT_TQ] + tot_diag)
            w_diag = jnp.where(causal, jnp.exp2(log_beta[:, diag] + between[ATT_TQ:]), 0.0)
            weights.append(jnp.concatenate([w_prev, w_diag], axis=1).astype(BF16))
        for a in range(ATT_HEADS_PER_STEP):
            acc_ref[a] = jnp.dot(weights[a], v_ref[pl.ds(off, 2 * ATT_TK), heads[a]],
                                 preferred_element_type=F32)

        def more(state):
            j, max_carry = state
            return jnp.logical_and(j >= 0, max_carry > -EXP2_UNDERFLOW)

        def sweep(state):
            j, _ = state
            for a in range(ATT_HEADS_PER_STEP):
                pv, tot = tile(a, j, carry_ref[a], None)
                acc_ref[a] += pv
                carry_ref[a] += tot
            return j - 1, jnp.max(carry_ref[...])

        lax.while_loop(more, sweep, (i - 2, jnp.max(carry_ref[...])))
        for a in range(ATT_HEADS_PER_STEP):
            o_ref[:, heads[a]] = acc_ref[a].astype(o_ref.dtype)


def _attention(qkv, q_col, c, w_ada, b_ada, ada_col):
    s = qkv.shape[0]
    assert ATT_TQ == ATT_TK and s % ATT_TQ == 0
    width = ATT_HEADS_PER_STEP * SB_HEAD_DIM
    groups = SB_HEADS // ATT_HEADS_PER_STEP
    g0 = q_col // width
    nq = s // ATT_TQ
    d, n_ada = w_ada.shape
    ada_tn = (n_ada - ada_col) // (groups * nq)
    assert ada_tn % LANES == 0 and ada_col % ada_tn == 0 and ada_tn * groups * nq == n_ada - ada_col
    a0 = ada_col // ada_tn
    return pl.pallas_call(
        _attn_kernel,
        grid=(groups, nq),
        in_specs=[pl.BlockSpec((ATT_TQ, width), lambda h, i: (i, g0 + h)),
                  pl.BlockSpec((s, width), lambda h, i: (0, g0 + groups + h)),
                  pl.BlockSpec((s, width), lambda h, i: (0, g0 + 2 * groups + h)),
                  pl.BlockSpec((1, d), lambda h, i: (0, 0)),
                  pl.BlockSpec((d, ada_tn), lambda h, i: (0, a0 + h * nq + i)),
                  pl.BlockSpec((1, ada_tn), lambda h, i: (0, a0 + h * nq + i))],
        out_specs=[pl.BlockSpec((ATT_TQ, width), lambda h, i: (i, h)),
                   pl.BlockSpec((1, ada_tn), lambda h, i: (0, h * nq + i))],
        out_shape=[jax.ShapeDtypeStruct((s, SB_WIDTH), BF16),
                   jax.ShapeDtypeStruct((1, n_ada - ada_col), F32)],
        scratch_shapes=[pltpu.VMEM((ATT_HEADS_PER_STEP, ATT_TQ, SB_HEAD_DIM), F32),
                        pltpu.VMEM((ATT_HEADS_PER_STEP, ATT_TQ, 1), F32)],
        compiler_params=_params(("arbitrary", "arbitrary")),
        name="sb_attention",
    )(qkv, qkv, qkv, c, w_ada, b_ada)


def kernel(x, c, w_ada, b_ada, g_pre_mix, w_in, g_v, w_s, b_s, w_proj_a, w_proj_b, w_o,
           g_post_mix, g_pre_mlp, w_ff1, w_ff2, g_post_mlp):
    batch, seq, d = x.shape
    assert batch == 1 and d == D_MODEL
    depth = w_ada.shape[0]
    xs = x.reshape(seq, d)
    q_scale = LOG2_E / math.sqrt(SB_HEAD_DIM)
    tm_in, tn_in = 1024, 512
    q_col = 2 * GMLP_WIDTH
    gate_col = q_col + 3 * SB_WIDTH
    t_q, t_k, t_gate, t_end = [c // tn_in for c in (q_col, q_col + SB_WIDTH, gate_col, gate_col + 2 * d)]
    in_epilogues = ((0, t_q, _gelu),
                    (t_q, t_k, lambda p: p * q_scale),
                    (t_k, t_gate, lambda p: p),
                    (t_gate, t_end, _sigmoid))

    for l in range(depth):
        row = lambda v: v[None, :]
        mod_head = _adaln(c, w_ada[l], row(b_ada[l]), 2 * d)
        sh1, sc1 = mod_head[:, :d], mod_head[:, d:]

        h = _prenorm(xs, row(g_pre_mix[l]), sc1, sh1)
        proj, wa, wb, wo, wf1 = _matmul(
            h, w_in[l], epilogues=in_epilogues, out_dtype=BF16, tm=tm_in, tn=tn_in, name="in_proj",
            casts=(w_proj_a[l], w_proj_b[l], w_o[l], w_ff1[l]))
        ga = _gmlp(proj, row(g_v[l]), w_s[l], b_s[l])
        ob, mod_tail = _attention(proj, q_col, c, w_ada[l], row(b_ada[l]), 2 * d)
        gt1, sh2, sc2, gt2 = [mod_tail[:, n * d:(n + 1) * d] for n in range(N_MOD - 2)]
        mixin = _merge(ga, ob, wa, wb, proj, gate_col)
        mix = _matmul(mixin, wo, epilogues=((0, d // 1024, lambda p: p),),
                      out_dtype=BF16, tm=1024, tn=1024, name="w_o")
        xs, h2 = _post_pre(xs, mix, row(g_post_mix[l]), gt1, row(g_pre_mlp[l]), sc2, sh2)

        hid = _matmul(h2, wf1, epilogues=((0, D_FF // 1024, lambda p: jnp.square(jnp.maximum(p, 0.0))),),
                      out_dtype=BF16, tm=1024, tn=1024, name="ff1")
        ff = _matmul_ktiled(hid, w_ff2[l], out_dtype=BF16, tm=2048, tn=1024, tk=2048, name="ff2")
        xs = _post(xs, ff, row(g_post_mlp[l]), gt2)
    return xs.reshape(batch, seq, d)
```

```python
import functools
import math

import jax
import jax.numpy as jnp
from jax import lax
from jax.experimental import pallas as pl
from jax.experimental.pallas import tpu as pltpu

F32 = jnp.float32
BF16 = jnp.bfloat16

D_MODEL = 4096
CHUNK = 64
GMLP_BLOCK = 128
GMLP_GROUPS = 16
GMLP_WIDTH = D_MODEL // 2
SB_HEAD_DIM = 128
SB_HEADS = (D_MODEL // 2) // SB_HEAD_DIM
SB_WIDTH = SB_HEADS * SB_HEAD_DIM
D_FF = 4 * D_MODEL
N_MOD = 6
EPS = 1e-6
LANES = 128
MIB = 1024 * 1024
VMEM_LIMIT = 56 * MIB

ATT_TQ = 256
ATT_TK = 256
ATT_HEADS_PER_STEP = 4
LOG2_E = 1.0 / math.log(2.0)
EXP2_UNDERFLOW = 160.0


def _params(semantics, vmem=VMEM_LIMIT):
    return pltpu.CompilerParams(dimension_semantics=semantics, vmem_limit_bytes=vmem)


def _ada_kernel(c_ref, w_ref, b_ref, o_ref):
    c = c_ref[...]
    s = c * jax.nn.sigmoid(c)
    s8 = jnp.broadcast_to(s, (8, s.shape[1])).astype(BF16)
    p = jnp.dot(s8, w_ref[...].astype(BF16), preferred_element_type=F32)
    o_ref[...] = p[0:1, :] + b_ref[...]


def _adaln(c, w, b, n_out, tn=512):
    d = w.shape[0]
    return pl.pallas_call(
        _ada_kernel,
        grid=(n_out // tn,),
        in_specs=[pl.BlockSpec((1, d), lambda j: (0, 0)),
                  pl.BlockSpec((d, tn), lambda j: (0, j)),
                  pl.BlockSpec((1, tn), lambda j: (0, j))],
        out_specs=pl.BlockSpec((1, tn), lambda j: (0, j)),
        out_shape=jax.ShapeDtypeStruct((1, n_out), F32),
        compiler_params=_params(("arbitrary",)),
        name="adaln",
    )(c, w, b)


def _rms(x, g):
    inv = lax.rsqrt(jnp.mean(x * x, axis=-1, keepdims=True) + EPS)
    return x * inv * g


def _prenorm_kernel(x_ref, g_ref, sc_ref, sh_ref, o_ref):
    h = _rms(x_ref[...], g_ref[...]) * (1.0 + sc_ref[...]) + sh_ref[...]
    o_ref[...] = h.astype(o_ref.dtype)


def _prenorm(x, g, sc, sh, tm=512):
    m, d = x.shape
    row = pl.BlockSpec((tm, d), lambda i: (i, 0))
    vec = pl.BlockSpec((1, d), lambda i: (0, 0))
    return pl.pallas_call(
        _prenorm_kernel,
        grid=(m // tm,),
        in_specs=[row, vec, vec, vec],
        out_specs=row,
        out_shape=jax.ShapeDtypeStruct((m, d), BF16),
        compiler_params=_params(("arbitrary",)),
        name="prenorm",
    )(x, g, sc, sh)


def _post_pre_kernel(x_ref, y_ref, gpost_ref, gate_ref, gpre_ref, sc_ref, sh_ref, x1_ref, h_ref):
    x1 = x_ref[...] + gate_ref[...] * _rms(y_ref[...].astype(F32), gpost_ref[...])
    x1_ref[...] = x1
    h = _rms(x1, gpre_ref[...]) * (1.0 + sc_ref[...]) + sh_ref[...]
    h_ref[...] = h.astype(h_ref.dtype)


def _post_pre(x, y, gpost, gate, gpre, sc, sh, tm=256):
    m, d = x.shape
    row = pl.BlockSpec((tm, d), lambda i: (i, 0))
    vec = pl.BlockSpec((1, d), lambda i: (0, 0))
    return pl.pallas_call(
        _post_pre_kernel,
        grid=(m // tm,),
        in_specs=[row, row, vec, vec, vec, vec, vec],
        out_specs=[row, row],
        out_shape=[jax.ShapeDtypeStruct((m, d), F32), jax.ShapeDtypeStruct((m, d), BF16)],
        compiler_params=_params(("arbitrary",)),
        name="post_pre",
    )(x, y, gpost, gate, gpre, sc, sh)


def _post_kernel(x_ref, y_ref, gpost_ref, gate_ref, o_ref):
    o_ref[...] = x_ref[...] + gate_ref[...] * _rms(y_ref[...].astype(F32), gpost_ref[...])


def _post(x, y, gpost, gate, tm=512):
    m, d = x.shape
    row = pl.BlockSpec((tm, d), lambda i: (i, 0))
    vec = pl.BlockSpec((1, d), lambda i: (0, 0))
    return pl.pallas_call(
        _post_kernel,
        grid=(m // tm,),
        in_specs=[row, row, vec, vec],
        out_specs=row,
        out_shape=jax.ShapeDtypeStruct((m, d), F32),
        compiler_params=_params(("arbitrary",)),
        name="post",
    )(x, y, gpost, gate)


def _gelu(p):
    return 0.5 * p * (1.0 + lax.erf(p * (1.0 / math.sqrt(2.0))))


def _sigmoid(p):
    return 0.5 * jnp.tanh(0.5 * p) + 0.5


def _mm_kernel(a_ref, w_ref, *refs, epilogues, n_casts):
    cast_src = refs[:n_casts]
    o_ref = refs[n_casts]
    cast_dst = refs[n_casts + 1:]
    j = pl.program_id(1)

    def run(fn):
        p = jnp.dot(a_ref[...], w_ref[...].astype(BF16), preferred_element_type=F32)
        o_ref[...] = fn(p).astype(o_ref.dtype)
        for src, dst in zip(cast_src, cast_dst):
            dst[...] = src[...].astype(dst.dtype)

    if len(epilogues) == 1:
        run(epilogues[0][2])
        return
    for lo, hi, fn in epilogues:
        pl.when(jnp.logical_and(j >= lo, j < hi))(functools.partial(run, fn))


BF16_ROWS = 16


def _matmul(a, w, *, epilogues, out_dtype, tm, tn, name, casts=(), a_buffers=2):
    m, k = a.shape
    n = w.shape[1]
    ni, nj = m // tm, n // tn
    assert epilogues[0][0] == 0 and epilogues[-1][1] == nj
    cast_specs, cast_shapes = [], []
    for cw in casts:
        r, c = cw.shape
        rows = BF16_ROWS * pl.cdiv(r, BF16_ROWS * ni * nj)
        assert r % rows == 0
        last = r // rows - 1
        cast_specs.append(pl.BlockSpec((rows, c), lambda i, j, last=last: (jnp.minimum(i * nj + j, last), 0)))
        cast_shapes.append(jax.ShapeDtypeStruct((r, c), BF16))
    res = pl.pallas_call(
        functools.partial(_mm_kernel, epilogues=epilogues, n_casts=len(casts)),
        grid=(ni, nj),
        in_specs=[pl.BlockSpec((tm, k), lambda i, j: (i, 0), pipeline_mode=pl.Buffered(a_buffers)),
                  pl.BlockSpec((k, tn), lambda i, j: (0, j))] + cast_specs,
        out_specs=[pl.BlockSpec((tm, tn), lambda i, j: (i, j))] + cast_specs,
        out_shape=[jax.ShapeDtypeStruct((m, n), out_dtype)] + cast_shapes,
        compiler_params=_params(("arbitrary", "arbitrary")),
        name=name,
    )(a, w, *casts)
    return res if casts else res[0]


MMK_ROWS = 256


def _mmk_kernel(a_ref, w_ref, o_ref, acc_ref):
    k = pl.program_id(2)
    last = pl.num_programs(2) - 1

    def chunks(store):
        w = w_ref[...].astype(BF16)
        for r in range(a_ref.shape[0] // MMK_ROWS):
            rows = pl.ds(r * MMK_ROWS, MMK_ROWS)
            store(rows, jnp.dot(a_ref[rows, :], w, preferred_element_type=F32))

    def first(rows, p):
        acc_ref[rows, :] = p

    def middle(rows, p):
        acc_ref[rows, :] += p

    def final(rows, p):
        o_ref[rows, :] = (acc_ref[rows, :] + p).astype(o_ref.dtype)

    pl.when(k == 0)(functools.partial(chunks, first))
    pl.when(jnp.logical_and(k > 0, k < last))(functools.partial(chunks, middle))
    pl.when(k == last)(functools.partial(chunks, final))


def _matmul_ktiled(a, w, *, out_dtype, tm, tn, tk, name):
    m, k = a.shape
    n = w.shape[1]
    assert k // tk >= 2
    return pl.pallas_call(
        _mmk_kernel,
        grid=(m // tm, n // tn, k // tk),
        in_specs=[pl.BlockSpec((tm, tk), lambda i, j, kk: (i, kk)),
                  pl.BlockSpec((tk, tn), lambda i, j, kk: (kk, j))],
        out_specs=pl.BlockSpec((tm, tn), lambda i, j, kk: (i, j)),
        out_shape=jax.ShapeDtypeStruct((m, n), out_dtype),
        scratch_shapes=[pltpu.VMEM((tm, tn), F32)],
        compiler_params=_params(("arbitrary", "arbitrary", "arbitrary")),
        name=name,
    )(a, w)


def _merge_kernel(ga_ref, ob_ref, wa_ref, wb_ref, sa_ref, sb_ref, o_ref):
    ya = jnp.dot(ga_ref[...], wa_ref[...].astype(BF16), preferred_element_type=F32)
    yb = jnp.dot(ob_ref[...], wb_ref[...].astype(BF16), preferred_element_type=F32)
    mix = sa_ref[...].astype(F32) * ya + sb_ref[...].astype(F32) * yb
    o_ref[...] = mix.astype(o_ref.dtype)


def _merge(ga, ob, wa, wb, proj, gate_col, tm=1024, tn=1024):
    m, k = ga.shape
    n = wa.shape[1]
    nb = n // tn
    ja = gate_col // tn
    act = pl.BlockSpec((tm, k), lambda i, j: (i, 0))
    wsp = pl.BlockSpec((k, tn), lambda i, j: (0, j))
    return pl.pallas_call(
        _merge_kernel,
        grid=(m // tm, nb),
        in_specs=[act, act, wsp, wsp,
                  pl.BlockSpec((tm, tn), lambda i, j: (i, j + ja)),
                  pl.BlockSpec((tm, tn), lambda i, j: (i, j + ja + nb))],
        out_specs=pl.BlockSpec((tm, tn), lambda i, j: (i, j)),
        out_shape=jax.ShapeDtypeStruct((m, n), BF16),
        compiler_params=_params(("arbitrary", "arbitrary")),
        name="merge",
    )(ga, ob, wa, wb, proj, proj)


def _gmlp_kernel(u_ref, v_ref, gv_ref, ws_ref, bs_ref, o_ref):
    rows = u_ref.shape[0]
    v = v_ref[...].astype(F32)
    mu = jnp.mean(v, axis=-1, keepdims=True)
    vc = v - mu
    inv = lax.rsqrt(jnp.mean(vc * vc, axis=-1, keepdims=True) + EPS)
    vn = (vc * inv * gv_ref[...]).astype(BF16)
    t = lax.broadcasted_iota(jnp.int32, (GMLP_BLOCK, GMLP_BLOCK), 0)
    s = lax.broadcasted_iota(jnp.int32, (GMLP_BLOCK, GMLP_BLOCK), 1)
    chunk_causal = (t // CHUNK) >= (s // CHUNK)
    gd = GMLP_WIDTH // GMLP_GROUPS
    for g in range(GMLP_GROUPS):
        w = jnp.where(chunk_causal, ws_ref[g], 0.0).astype(BF16)
        bias = bs_ref[g]
        for b in range(rows // GMLP_BLOCK):
            r = slice(b * GMLP_BLOCK, (b + 1) * GMLP_BLOCK)
            c = slice(g * gd, (g + 1) * gd)
            y = jnp.dot(w, vn[r, c], preferred_element_type=F32) + bias
            o_ref[r, c] = (u_ref[r, c].astype(F32) * y).astype(o_ref.dtype)


def _gmlp(zuv, g_v, w_s, b_s, rows=1024):
    m = zuv.shape[0]
    half = pl.BlockSpec((rows, GMLP_WIDTH), lambda i: (i, 0))
    return pl.pallas_call(
        _gmlp_kernel,
        grid=(m // rows,),
        in_specs=[half,
                  pl.BlockSpec((rows, GMLP_WIDTH), lambda i: (i, 1)),
                  pl.BlockSpec((1, GMLP_WIDTH), lambda i: (0, 0)),
                  pl.BlockSpec((GMLP_GROUPS, GMLP_BLOCK, GMLP_BLOCK), lambda i: (0, 0, 0)),
                  pl.BlockSpec((GMLP_GROUPS, GMLP_BLOCK, 1), lambda i: (0, 0, 0))],
        out_specs=half,
        out_shape=jax.ShapeDtypeStruct((m, GMLP_WIDTH), BF16),
        compiler_params=_params(("arbitrary",)),
        name="gmlp",
    )(zuv, zuv, g_v, w_s, b_s[:, :, None])


def _sb_scores(q, k):
    return lax.dot_general(q, k, (((1,), (1,)), ((), ())), preferred_element_type=F32)


def _sb_logits(z):
    e = jnp.exp2(-jnp.abs(z))
    log_beta = jnp.minimum(z, 0.0) - jnp.log(1.0 + e) * LOG2_E
    return log_beta, log_beta - z


def _hi_lo(x):
    hi = x.astype(BF16)
    lo = (x - hi.astype(F32)).astype(BF16)
    return jnp.concatenate([hi, lo], axis=1)


def _sb_block(q, kj, vj, carry, upper2, mask):
    log_beta, lom = _sb_logits(_sb_scores(q, kj))
    if mask is not None:
        lom = jnp.where(mask, lom, 0.0)
    between = jnp.dot(_hi_lo(lom), upper2, preferred_element_type=F32)
    a = jnp.exp2(log_beta + between + carry)
    if mask is not None:
        a = jnp.where(mask, a, 0.0)
    pv = jnp.dot(a.astype(BF16), vj, preferred_element_type=F32)
    return pv, jnp.sum(lom, axis=1, keepdims=True)


def _attn_kernel(q_ref, k_ref, v_ref, c_ref, wada_ref, bada_ref, o_ref, mod_ref, acc_ref, carry_ref):
    i = pl.program_id(1)
    ada_tail = functools.partial(_ada_kernel, c_ref, wada_ref, bada_ref, mod_ref)
    row = lax.broadcasted_iota(jnp.int32, (ATT_TQ, ATT_TK), 0)
    col = lax.broadcasted_iota(jnp.int32, (ATT_TQ, ATT_TK), 1)
    upper = (row > col).astype(BF16)
    upper2 = jnp.concatenate([upper, upper], axis=0)
    causal = col < row
    heads = [slice(a * SB_HEAD_DIM, (a + 1) * SB_HEAD_DIM) for a in range(ATT_HEADS_PER_STEP)]

    def tile(a, j, carry, mask):
        off = pl.multiple_of(j * ATT_TK, ATT_TK)
        return _sb_block(q_ref[:, heads[a]], k_ref[pl.ds(off, ATT_TK), heads[a]],
                         v_ref[pl.ds(off, ATT_TK), heads[a]], carry, upper2, mask)

    no_carry = jnp.zeros((ATT_TQ, 1), F32)

    @pl.when(i == 0)
    def _():
        ada_tail()
        for a in range(ATT_HEADS_PER_STEP):
            pv, _ = tile(a, i, no_carry, causal)
            o_ref[:, heads[a]] = pv.astype(o_ref.dtype)

    @pl.when(i > 0)
    def _():
        ada_tail()
        off = pl.multiple_of((i - 1) * ATT_TK, ATT_TK)
        prev, diag = slice(0, ATT_TK), slice(ATT_TK, 2 * ATT_TK)
        scores = [_sb_scores(q_ref[:, heads[a]], k_ref[pl.ds(off, 2 * ATT_TK), heads[a]])
                  for a in range(ATT_HEADS_PER_STEP)]
        stage1 = []
        for a in range(ATT_HEADS_PER_STEP):
            log_beta, lom = _sb_logits(scores[a])
            lom_prev = lom[:, prev]
            lom_diag = jnp.where(causal, lom[:, diag], 0.0)
            tot_diag = jnp.sum(lom_diag, axis=1, keepdims=True)
            carry_ref[a] = tot_diag + jnp.sum(lom_prev, axis=1, keepdims=True)
            stage1.append((log_beta, jnp.concatenate([_hi_lo(lom_prev), _hi_lo(lom_diag)], axis=0), tot_diag))
        stage2 = [jnp.dot(hl, upper2, preferred_element_type=F32) for _, hl, _ in stage1]
        weights = []
        for a in range(ATT_HEADS_PER_STEP):
            log_beta, _, tot_diag = stage1[a]
            between = stage2[a]
            w_prev = jnp.exp2(log_beta[:, prev] + between[:ATT_TQ] + tot_diag)
            w_diag = jnp.where(causal, jnp.exp2(log_beta[:, diag] + between[ATT_TQ:]), 0.0)
            weights.append(jnp.concatenate([w_prev, w_diag], axis=1).astype(BF16))
        for a in range(ATT_HEADS_PER_STEP):
            acc_ref[a] = jnp.dot(weights[a], v_ref[pl.ds(off, 2 * ATT_TK), heads[a]],
                                 preferred_element_type=F32)

        def more(state):
            j, max_carry = state
            return jnp.logical_and(j >= 0, max_carry > -EXP2_UNDERFLOW)

        def sweep(state):
            j, _ = state
            for a in range(ATT_HEADS_PER_STEP):
                pv, tot = tile(a, j, carry_ref[a], None)
                acc_ref[a] += pv
                carry_ref[a] += tot
            return j - 1, jnp.max(carry_ref[...])

        lax.while_loop(more, sweep, (i - 2, jnp.max(carry_ref[...])))
        for a in range(ATT_HEADS_PER_STEP):
            o_ref[:, heads[a]] = acc_ref[a].astype(o_ref.dtype)


def _attention(qkv, q_col, c, w_ada, b_ada, ada_col):
    s = qkv.shape[0]
    assert ATT_TQ == ATT_TK and s % ATT_TQ == 0
    width = ATT_HEADS_PER_STEP * SB_HEAD_DIM
    groups = SB_HEADS // ATT_HEADS_PER_STEP
    g0 = q_col // width
    nq = s // ATT_TQ
    d, n_ada = w_ada.shape
    ada_tn = (n_ada - ada_col) // (groups * nq)
    assert ada_tn % LANES == 0 and ada_col % ada_tn == 0 and ada_tn * groups * nq == n_ada - ada_col
    a0 = ada_col // ada_tn
    return pl.pallas_call(
        _attn_kernel,
        grid=(groups, nq),
        in_specs=[pl.BlockSpec((ATT_TQ, width), lambda h, i: (i, g0 + h)),
                  pl.BlockSpec((s, width), lambda h, i: (0, g0 + groups + h)),
                  pl.BlockSpec((s, width), lambda h, i: (0, g0 + 2 * groups + h)),
                  pl.BlockSpec((1, d), lambda h, i: (0, 0)),
                  pl.BlockSpec((d, ada_tn), lambda h, i: (0, a0 + h * nq + i)),
                  pl.BlockSpec((1, ada_tn), lambda h, i: (0, a0 + h * nq + i))],
        out_specs=[pl.BlockSpec((ATT_TQ, width), lambda h, i: (i, h)),
                   pl.BlockSpec((1, ada_tn), lambda h, i: (0, h * nq + i))],
        out_shape=[jax.ShapeDtypeStruct((s, SB_WIDTH), BF16),
                   jax.ShapeDtypeStruct((1, n_ada - ada_col), F32)],
        scratch_shapes=[pltpu.VMEM((ATT_HEADS_PER_STEP, ATT_TQ, SB_HEAD_DIM), F32),
                        pltpu.VMEM((ATT_HEADS_PER_STEP, ATT_TQ, 1), F32)],
        compiler_params=_params(("arbitrary", "arbitrary")),
        name="sb_attention",
    )(qkv, qkv, qkv, c, w_ada, b_ada)


def kernel(x, c, w_ada, b_ada, g_pre_mix, w_in, g_v, w_s, b_s, w_proj_a, w_proj_b, w_o,
           g_post_mix, g_pre_mlp, w_ff1, w_ff2, g_post_mlp):
    batch, seq, d = x.shape
    assert batch == 1 and d == D_MODEL
    depth = w_ada.shape[0]
    xs = x.reshape(seq, d)
    q_scale = LOG2_E / math.sqrt(SB_HEAD_DIM)
    tm_in, tn_in = 2048, 512
    q_col = 2 * GMLP_WIDTH
    gate_col = q_col + 3 * SB_WIDTH
    t_q, t_k, t_gate, t_end = [c // tn_in for c in (q_col, q_col + SB_WIDTH, gate_col, gate_col + 2 * d)]
    in_epilogues = ((0, t_q, _gelu),
                    (t_q, t_k, lambda p: p * q_scale),
                    (t_k, t_gate, lambda p: p),
                    (t_gate, t_end, _sigmoid))

    for l in range(depth):
        row = lambda v: v[None, :]
        mod_head = _adaln(c, w_ada[l], row(b_ada[l]), 2 * d)
        sh1, sc1 = mod_head[:, :d], mod_head[:, d:]

        h = _prenorm(xs, row(g_pre_mix[l]), sc1, sh1)
        proj, wa, wb, wo, wf1 = _matmul(
            h, w_in[l], epilogues=in_epilogues, out_dtype=BF16, tm=tm_in, tn=tn_in, name="in_proj",
            casts=(w_proj_a[l], w_proj_b[l], w_o[l], w_ff1[l]), a_buffers=1)
        ga = _gmlp(proj, row(g_v[l]), w_s[l], b_s[l])
        ob, mod_tail = _attention(proj, q_col, c, w_ada[l], row(b_ada[l]), 2 * d)
        gt1, sh2, sc2, gt2 = [mod_tail[:, n * d:(n + 1) * d] for n in range(N_MOD - 2)]
        mixin = _merge(ga, ob, wa, wb, proj, gate_col)
        mix = _matmul(mixin, wo, epilogues=((0, d // 1024, lambda p: p),),
                      out_dtype=BF16, tm=1024, tn=1024, name="w_o")
        xs, h2 = _post_pre(xs, mix, row(g_post_mix[l]), gt1, row(g_pre_mlp[l]), sc2, sh2)

        hid = _matmul(h2, wf1, epilogues=((0, D_FF // 1024, lambda p: jnp.square(jnp.maximum(p, 0.0))),),
                      out_dtype=BF16, tm=1024, tn=1024, name="ff1")
        ff = _matmul_ktiled(hid, w_ff2[l], out_dtype=BF16, tm=2048, tn=1024, tk=2048, name="ff2")
        xs = _post(xs, ff, row(g_post_mlp[l]), gt2)
    return xs.reshape(batch, seq, d)
```

```python
import functools
import math

import jax
import jax.numpy as jnp
from jax import lax
from jax.experimental import pallas as pl
from jax.experimental.pallas import tpu as pltpu

F32 = jnp.float32
BF16 = jnp.bfloat16

D_MODEL = 4096
CHUNK = 64
GMLP_BLOCK = 128
GMLP_GROUPS = 16
GMLP_WIDTH = D_MODEL // 2
SB_HEAD_DIM = 128
SB_HEADS = (D_MODEL // 2) // SB_HEAD_DIM
SB_WIDTH = SB_HEADS * SB_HEAD_DIM
D_FF = 4 * D_MODEL
N_MOD = 6
EPS = 1e-6
LANES = 128
MIB = 1024 * 1024
VMEM_LIMIT = 56 * MIB

ATT_TQ = 256
ATT_TK = 256
ATT_HEADS_PER_STEP = 4
LOG2_E = 1.0 / math.log(2.0)
EXP2_UNDERFLOW = 160.0


def _params(semantics, vmem=VMEM_LIMIT):
    return pltpu.CompilerParams(dimension_semantics=semantics, vmem_limit_bytes=vmem)


def _ada_kernel(c_ref, w_ref, b_ref, o_ref):
    c = c_ref[...]
    s = c * jax.nn.sigmoid(c)
    s8 = jnp.broadcast_to(s, (8, s.shape[1])).astype(BF16)
    p = jnp.dot(s8, w_ref[...].astype(BF16), preferred_element_type=F32)
    o_ref[...] = p[0:1, :] + b_ref[...]


def _adaln(c, w, b, n_out, tn=512):
    d = w.shape[0]
    return pl.pallas_call(
        _ada_kernel,
        grid=(n_out // tn,),
        in_specs=[pl.BlockSpec((1, d), lambda j: (0, 0)),
                  pl.BlockSpec((d, tn), lambda j: (0, j)),
                  pl.BlockSpec((1, tn), lambda j: (0, j))],
        out_specs=pl.BlockSpec((1, tn), lambda j: (0, j)),
        out_shape=jax.ShapeDtypeStruct((1, n_out), F32),
        compiler_params=_params(("arbitrary",)),
        name="adaln",
    )(c, w, b)


def _rms(x, g):
    inv = lax.rsqrt(jnp.mean(x * x, axis=-1, keepdims=True) + EPS)
    return x * inv * g


def _prenorm_kernel(x_ref, g_ref, sc_ref, sh_ref, o_ref):
    h = _rms(x_ref[...], g_ref[...]) * (1.0 + sc_ref[...]) + sh_ref[...]
    o_ref[...] = h.astype(o_ref.dtype)


def _prenorm(x, g, sc, sh, tm=512):
    m, d = x.shape
    row = pl.BlockSpec((tm, d), lambda i: (i, 0))
    vec = pl.BlockSpec((1, d), lambda i: (0, 0))
    return pl.pallas_call(
        _prenorm_kernel,
        grid=(m // tm,),
        in_specs=[row, vec, vec, vec],
        out_specs=row,
        out_shape=jax.ShapeDtypeStruct((m, d), BF16),
        compiler_params=_params(("arbitrary",)),
        name="prenorm",
    )(x, g, sc, sh)


def _post_pre_kernel(x_ref, y_ref, gpost_ref, gate_ref, gpre_ref, sc_ref, sh_ref, x1_ref, h_ref):
    x1 = x_ref[...] + gate_ref[...] * _rms(y_ref[...].astype(F32), gpost_ref[...])
    x1_ref[...] = x1
    h = _rms(x1, gpre_ref[...]) * (1.0 + sc_ref[...]) + sh_ref[...]
    h_ref[...] = h.astype(h_ref.dtype)


def _post_pre(x, y, gpost, gate, gpre, sc, sh, tm=256):
    m, d = x.shape
    row = pl.BlockSpec((tm, d), lambda i: (i, 0))
    vec = pl.BlockSpec((1, d), lambda i: (0, 0))
    return pl.pallas_call(
        _post_pre_kernel,
        grid=(m // tm,),
        in_specs=[row, row, vec, vec, vec, vec, vec],
        out_specs=[row, row],
        out_shape=[jax.ShapeDtypeStruct((m, d), F32), jax.ShapeDtypeStruct((m, d), BF16)],
        compiler_params=_params(("arbitrary",)),
        name="post_pre",
    )(x, y, gpost, gate, gpre, sc, sh)


def _post_kernel(x_ref, y_ref, gpost_ref, gate_ref, o_ref):
    o_ref[...] = x_ref[...] + gate_ref[...] * _rms(y_ref[...].astype(F32), gpost_ref[...])


def _post(x, y, gpost, gate, tm=512):
    m, d = x.shape
    row = pl.BlockSpec((tm, d), lambda i: (i, 0))
    vec = pl.BlockSpec((1, d), lambda i: (0, 0))
    return pl.pallas_call(
        _post_kernel,
        grid=(m // tm,),
        in_specs=[row, row, vec, vec],
        out_specs=row,
        out_shape=jax.ShapeDtypeStruct((m, d), F32),
        compiler_params=_params(("arbitrary",)),
        name="post",
    )(x, y, gpost, gate)


def _gelu(p):
    return 0.5 * p * (1.0 + lax.erf(p * (1.0 / math.sqrt(2.0))))


def _sigmoid(p):
    return 0.5 * jnp.tanh(0.5 * p) + 0.5


def _mm_kernel(a_ref, w_ref, *refs, epilogues, n_casts):
    cast_src = refs[:n_casts]
    o_ref = refs[n_casts]
    cast_dst = refs[n_casts + 1:]
    j = pl.program_id(1)

    def run(fn):
        p = jnp.dot(a_ref[...], w_ref[...].astype(BF16), preferred_element_type=F32)
        o_ref[...] = fn(p).astype(o_ref.dtype)
        for src, dst in zip(cast_src, cast_dst):
            dst[...] = src[...].astype(dst.dtype)

    if len(epilogues) == 1:
        run(epilogues[0][2])
        return
    for lo, hi, fn in epilogues:
        pl.when(jnp.logical_and(j >= lo, j < hi))(functools.partial(run, fn))


BF16_ROWS = 16


def _matmul(a, w, *, epilogues, out_dtype, tm, tn, name, casts=()):
    m, k = a.shape
    n = w.shape[1]
    ni, nj = m // tm, n // tn
    assert epilogues[0][0] == 0 and epilogues[-1][1] == nj
    cast_specs, cast_shapes = [], []
    for cw in casts:
        r, c = cw.shape
        rows = BF16_ROWS * pl.cdiv(r, BF16_ROWS * ni * nj)
        assert r % rows == 0
        last = r // rows - 1
        cast_specs.append(pl.BlockSpec((rows, c), lambda i, j, last=last: (jnp.minimum(i * nj + j, last), 0)))
        cast_shapes.append(jax.ShapeDtypeStruct((r, c), BF16))
    res = pl.pallas_call(
        functools.partial(_mm_kernel, epilogues=epilogues, n_casts=len(casts)),
        grid=(ni, nj),
        in_specs=[pl.BlockSpec((tm, k), lambda i, j: (i, 0)),
                  pl.BlockSpec((k, tn), lambda i, j: (0, j))] + cast_specs,
        out_specs=[pl.BlockSpec((tm, tn), lambda i, j: (i, j))] + cast_specs,
        out_shape=[jax.ShapeDtypeStruct((m, n), out_dtype)] + cast_shapes,
        compiler_params=_params(("arbitrary", "arbitrary")),
        name=name,
    )(a, w, *casts)
    return res if casts else res[0]


MMK_ROWS = 256


def _mmk_kernel(a_ref, w_ref, o_ref, acc_ref):
    k = pl.program_id(2)
    last = pl.num_programs(2) - 1

    def chunks(store):
        w = w_ref[...].astype(BF16)
        for r in range(a_ref.shape[0] // MMK_ROWS):
            rows = pl.ds(r * MMK_ROWS, MMK_ROWS)
            store(rows, jnp.dot(a_ref[rows, :], w, preferred_element_type=F32))

    def first(rows, p):
        acc_ref[rows, :] = p

    def middle(rows, p):
        acc_ref[rows, :] += p

    def final(rows, p):
        o_ref[rows, :] = (acc_ref[rows, :] + p).astype(o_ref.dtype)

    pl.when(k == 0)(functools.partial(chunks, first))
    pl.when(jnp.logical_and(k > 0, k < last))(functools.partial(chunks, middle))
    pl.when(k == last)(functools.partial(chunks, final))


def _matmul_ktiled(a, w, *, out_dtype, tm, tn, tk, name):
    m, k = a.shape
    n = w.shape[1]
    assert k // tk >= 2
    return pl.pallas_call(
        _mmk_kernel,
        grid=(m // tm, n // tn, k // tk),
        in_specs=[pl.BlockSpec((tm, tk), lambda i, j, kk: (i, kk)),
                  pl.BlockSpec((tk, tn), lambda i, j, kk: (kk, j))],
        out_specs=pl.BlockSpec((tm, tn), lambda i, j, kk: (i, j)),
        out_shape=jax.ShapeDtypeStruct((m, n), out_dtype),
        scratch_shapes=[pltpu.VMEM((tm, tn), F32)],
        compiler_params=_params(("arbitrary", "arbitrary", "arbitrary")),
        name=name,
    )(a, w)


def _merge_kernel(ga_ref, ob_ref, wa_ref, wb_ref, sa_ref, sb_ref, o_ref):
    ya = jnp.dot(ga_ref[...], wa_ref[...].astype(BF16), preferred_element_type=F32)
    yb = jnp.dot(ob_ref[...], wb_ref[...].astype(BF16), preferred_element_type=F32)
    mix = sa_ref[...].astype(F32) * ya + sb_ref[...].astype(F32) * yb
    o_ref[...] = mix.astype(o_ref.dtype)


def _merge(ga, ob, wa, wb, proj, gate_col, tm=1024, tn=1024):
    m, k = ga.shape
    n = wa.shape[1]
    nb = n // tn
    ja = gate_col // tn
    act = pl.BlockSpec((tm, k), lambda i, j: (i, 0))
    wsp = pl.BlockSpec((k, tn), lambda i, j: (0, j))
    return pl.pallas_call(
        _merge_kernel,
        grid=(m // tm, nb),
        in_specs=[act, act, wsp, wsp,
                  pl.BlockSpec((tm, tn), lambda i, j: (i, j + ja)),
                  pl.BlockSpec((tm, tn), lambda i, j: (i, j + ja + nb))],
        out_specs=pl.BlockSpec((tm, tn), lambda i, j: (i, j)),
        out_shape=jax.ShapeDtypeStruct((m, n), BF16),
        compiler_params=_params(("arbitrary", "arbitrary")),
        name="merge",
    )(ga, ob, wa, wb, proj, proj)


def _gmlp_kernel(u_ref, v_ref, gv_ref, ws_ref, bs_ref, o_ref):
    rows = u_ref.shape[0]
    v = v_ref[...].astype(F32)
    mu = jnp.mean(v, axis=-1, keepdims=True)
    vc = v - mu
    inv = lax.rsqrt(jnp.mean(vc * vc, axis=-1, keepdims=True) + EPS)
    vn = (vc * inv * gv_ref[...]).astype(BF16)
    t = lax.broadcasted_iota(jnp.int32, (GMLP_BLOCK, GMLP_BLOCK), 0)
    s = lax.broadcasted_iota(jnp.int32, (GMLP_BLOCK, GMLP_BLOCK), 1)
    chunk_causal = (t // CHUNK) >= (s // CHUNK)
    gd = GMLP_WIDTH // GMLP_GROUPS
    for g in range(GMLP_GROUPS):
        w = jnp.where(chunk_causal, ws_ref[g], 0.0).astype(BF16)
        bias = bs_ref[g]
        for b in range(rows // GMLP_BLOCK):
            r = slice(b * GMLP_BLOCK, (b + 1) * GMLP_BLOCK)
            c = slice(g * gd, (g + 1) * gd)
            y = jnp.dot(w, vn[r, c], preferred_element_type=F32) + bias
            o_ref[r, c] = (u_ref[r, c].astype(F32) * y).astype(o_ref.dtype)


def _gmlp(zuv, g_v, w_s, b_s, rows=1024):
    m = zuv.shape[0]
    half = pl.BlockSpec((rows, GMLP_WIDTH), lambda i: (i, 0))
    return pl.pallas_call(
        _gmlp_kernel,
        grid=(m // rows,),
        in_specs=[half,
                  pl.BlockSpec((rows, GMLP_WIDTH), lambda i: (i, 1)),
                  pl.BlockSpec((1, GMLP_WIDTH), lambda i: (0, 0)),
                  pl.BlockSpec((GMLP_GROUPS, GMLP_BLOCK, GMLP_BLOCK), lambda i: (0, 0, 0)),
                  pl.BlockSpec((GMLP_GROUPS, GMLP_BLOCK, 1), lambda i: (0, 0, 0))],
        out_specs=half,
        out_shape=jax.ShapeDtypeStruct((m, GMLP_WIDTH), BF16),
        compiler_params=_params(("arbitrary",)),
        name="gmlp",
    )(zuv, zuv, g_v, w_s, b_s[:, :, None])


def _sb_scores(q, k):
    return lax.dot_general(q, k, (((1,), (1,)), ((), ())), preferred_element_type=F32)


def _sb_logits(z):
    e = jnp.exp2(-jnp.abs(z))
    log_beta = jnp.minimum(z, 0.0) - jnp.log(1.0 + e) * LOG2_E
    return log_beta, log_beta - z


def _hi_lo(x):
    hi = x.astype(BF16)
    lo = (x - hi.astype(F32)).astype(BF16)
    return jnp.concatenate([hi, lo], axis=1)


def _sb_block(q, kj, vj, carry, upper2, mask):
    log_beta, lom = _sb_logits(_sb_scores(q, kj))
    if mask is not None:
        lom = jnp.where(mask, lom, 0.0)
    between = jnp.dot(_hi_lo(lom), upper2, preferred_element_type=F32)
    a = jnp.exp2(log_beta + between + carry)
    if mask is not None:
        a = jnp.where(mask, a, 0.0)
    pv = jnp.dot(a.astype(BF16), vj, preferred_element_type=F32)
    return pv, jnp.sum(lom, axis=1, keepdims=True)


def _attn_kernel(q_ref, k_ref, v_ref, c_ref, wada_ref, bada_ref, o_ref, mod_ref, acc_ref, carry_ref):
    i = pl.program_id(1)
    ada_tail = functools.partial(_ada_kernel, c_ref, wada_ref, bada_ref, mod_ref)
    row = lax.broadcasted_iota(jnp.int32, (ATT_TQ, ATT_TK), 0)
    col = lax.broadcasted_iota(jnp.int32, (ATT_TQ, ATT_TK), 1)
    upper = (row > col).astype(BF16)
    upper2 = jnp.concatenate([upper, upper], axis=0)
    causal = col < row
    heads = [slice(a * SB_HEAD_DIM, (a + 1) * SB_HEAD_DIM) for a in range(ATT_HEADS_PER_STEP)]

    def tile(a, j, carry, mask):
        off = pl.multiple_of(j * ATT_TK, ATT_TK)
        return _sb_block(q_ref[:, heads[a]], k_ref[pl.ds(off, ATT_TK), heads[a]],
                         v_ref[pl.ds(off, ATT_TK), heads[a]], carry, upper2, mask)

    no_carry = jnp.zeros((ATT_TQ, 1), F32)

    @pl.when(i == 0)
    def _():
        ada_tail()
        for a in range(ATT_HEADS_PER_STEP):
            pv, _ = tile(a, i, no_carry, causal)
            o_ref[:, heads[a]] = pv.astype(o_ref.dtype)

    @pl.when(i > 0)
    def _():
        ada_tail()
        off = pl.multiple_of((i - 1) * ATT_TK, ATT_TK)
        prev, diag = slice(0, ATT_TK), slice(ATT_TK, 2 * ATT_TK)
        st = [dict() for _ in range(ATT_HEADS_PER_STEP)]

        def s_scores(a):
            st[a]["z"] = _sb_scores(q_ref[:, heads[a]], k_ref[pl.ds(off, 2 * ATT_TK), heads[a]])

        def s_logits(a):
            log_beta, lom = _sb_logits(st[a]["z"])
            lom_prev = lom[:, prev]
            lom_diag = jnp.where(causal, lom[:, diag], 0.0)
            tot_diag = jnp.sum(lom_diag, axis=1, keepdims=True)
            carry_ref[a] = tot_diag + jnp.sum(lom_prev, axis=1, keepdims=True)
            st[a].update(lb=log_beta, tot=tot_diag,
                         hl=jnp.concatenate([_hi_lo(lom_prev), _hi_lo(lom_diag)], axis=0))

        def s_cumsum(a):
            st[a]["bw"] = jnp.dot(st[a]["hl"], upper2, preferred_element_type=F32)

        def s_weights(a):
            lb, bw = st[a]["lb"], st[a]["bw"]
            w_prev = jnp.exp2(lb[:, prev] + bw[:ATT_TQ] + st[a]["tot"])
            w_diag = jnp.where(causal, jnp.exp2(lb[:, diag] + bw[ATT_TQ:]), 0.0)
            st[a]["w"] = jnp.concatenate([w_prev, w_diag], axis=1).astype(BF16)

        def s_out(a):
            acc_ref[a] = jnp.dot(st[a]["w"], v_ref[pl.ds(off, 2 * ATT_TK), heads[a]],
                                 preferred_element_type=F32)

        stages = [s_scores, s_logits, s_cumsum, s_weights, s_out]
        for t in range(ATT_HEADS_PER_STEP + len(stages) - 1):
            for a in range(ATT_HEADS_PER_STEP):
                if 0 <= t - a < len(stages):
                    stages[t - a](a)

        def more(state):
            j, max_carry = state
            return jnp.logical_and(j >= 0, max_carry > -EXP2_UNDERFLOW)

        def sweep(state):
            j, _ = state
            for a in range(ATT_HEADS_PER_STEP):
                pv, tot = tile(a, j, carry_ref[a], None)
                acc_ref[a] += pv
                carry_ref[a] += tot
            return j - 1, jnp.max(carry_ref[...])

        lax.while_loop(more, sweep, (i - 2, jnp.max(carry_ref[...])))
        for a in range(ATT_HEADS_PER_STEP):
            o_ref[:, heads[a]] = acc_ref[a].astype(o_ref.dtype)


def _attention(qkv, q_col, c, w_ada, b_ada, ada_col):
    s = qkv.shape[0]
    assert ATT_TQ == ATT_TK and s % ATT_TQ == 0
    width = ATT_HEADS_PER_STEP * SB_HEAD_DIM
    groups = SB_HEADS // ATT_HEADS_PER_STEP
    g0 = q_col // width
    nq = s // ATT_TQ
    d, n_ada = w_ada.shape
    ada_tn = (n_ada - ada_col) // (groups * nq)
    assert ada_tn % LANES == 0 and ada_col % ada_tn == 0 and ada_tn * groups * nq == n_ada - ada_col
    a0 = ada_col // ada_tn
    return pl.pallas_call(
        _attn_kernel,
        grid=(groups, nq),
        in_specs=[pl.BlockSpec((ATT_TQ, width), lambda h, i: (i, g0 + h)),
                  pl.BlockSpec((s, width), lambda h, i: (0, g0 + groups + h)),
                  pl.BlockSpec((s, width), lambda h, i: (0, g0 + 2 * groups + h)),
                  pl.BlockSpec((1, d), lambda h, i: (0, 0)),
                  pl.BlockSpec((d, ada_tn), lambda h, i: (0, a0 + h * nq + i)),
                  pl.BlockSpec((1, ada_tn), lambda h, i: (0, a0 + h * nq + i))],
        out_specs=[pl.BlockSpec((ATT_TQ, width), lambda h, i: (i, h)),
                   pl.BlockSpec((1, ada_tn), lambda h, i: (0, h * nq + i))],
        out_shape=[jax.ShapeDtypeStruct((s, SB_WIDTH), BF16),
                   jax.ShapeDtypeStruct((1, n_ada - ada_col), F32)],
        scratch_shapes=[pltpu.VMEM((ATT_HEADS_PER_STEP, ATT_TQ, SB_HEAD_DIM), F32),
                        pltpu.VMEM((ATT_HEADS_PER_STEP, ATT_TQ, 1), F32)],
        compiler_params=_params(("arbitrary", "arbitrary")),
        name="sb_attention",
    )(qkv, qkv, qkv, c, w_ada, b_ada)


def kernel(x, c, w_ada, b_ada, g_pre_mix, w_in, g_v, w_s, b_s, w_proj_a, w_proj_b, w_o,
           g_post_mix, g_pre_mlp, w_ff1, w_ff2, g_post_mlp):
    batch, seq, d = x.shape
    assert batch == 1 and d == D_MODEL
    depth = w_ada.shape[0]
    xs = x.reshape(seq, d)
    q_scale = LOG2_E / math.sqrt(SB_HEAD_DIM)
    tm_in, tn_in = 1024, 512
    q_col = 2 * GMLP_WIDTH
    gate_col = q_col + 3 * SB_WIDTH
    t_q, t_k, t_gate, t_end = [c // tn_in for c in (q_col, q_col + SB_WIDTH, gate_col, gate_col + 2 * d)]
    in_epilogues = ((0, t_q, _gelu),
                    (t_q, t_k, lambda p: p * q_scale),
                    (t_k, t_gate, lambda p: p),
                    (t_gate, t_end, _sigmoid))

    for l in range(depth):
        row = lambda v: v[None, :]
        mod_head = _adaln(c, w_ada[l], row(b_ada[l]), 2 * d)
        sh1, sc1 = mod_head[:, :d], mod_head[:, d:]

        h = _prenorm(xs, row(g_pre_mix[l]), sc1, sh1)
        proj, wa, wb, wo, wf1 = _matmul(
            h, w_in[l], epilogues=in_epilogues, out_dtype=BF16, tm=tm_in, tn=tn_in, name="in_proj",
            casts=(w_proj_a[l], w_proj_b[l], w_o[l], w_ff1[l]))
        ga = _gmlp(proj, row(g_v[l]), w_s[l], b_s[l])
        ob, mod_tail = _attention(proj, q_col, c, w_ada[l], row(b_ada[l]), 2 * d)
        gt1, sh2, sc2, gt2 = [mod_tail[:, n * d:(n + 1) * d] for n in range(N_MOD - 2)]
        mixin = _merge(ga, ob, wa, wb, proj, gate_col)
        mix = _matmul(mixin, wo, epilogues=((0, d // 1024, lambda p: p),),
                      out_dtype=BF16, tm=1024, tn=1024, name="w_o")
        xs, h2 = _post_pre(xs, mix, row(g_post_mix[l]), gt1, row(g_pre_mlp[l]), sc2, sh2)

        hid = _matmul(h2, wf1, epilogues=((0, D_FF // 1024, lambda p: jnp.square(jnp.maximum(p, 0.0))),),
                      out_dtype=BF16, tm=1024, tn=1024, name="ff1")
        ff = _matmul_ktiled(hid, w_ff2[l], out_dtype=BF16, tm=2048, tn=1024, tk=2048, name="ff2")
        xs = _post(xs, ff, row(g_post_mlp[l]), gt2)
    return xs.reshape(batch, seq, d)
```
